```python
import math
import jax, jax.numpy as jnp
from jax import lax
import numpy as np

D_MODEL = 2048
BATCH = 2
SEQ = 16384
DEPTH = 1

HEAD_DIM = 128
HEADS_PER_GROUP = 4
ATTN_PATTERNS = ((128, 1), (512, 4), (2048, 16))
N_GROUPS = len(ATTN_PATTERNS)
N_ATTN_HEADS = N_GROUPS * HEADS_PER_GROUP
ATTN_WIDTH = N_ATTN_HEADS * HEAD_DIM
ATTN_OUT_WIDTH = HEADS_PER_GROUP * HEAD_DIM
ATTN_BLOCK = 128
ROPE_THETA = 500000.0
ROPE_DIM = HEAD_DIM // 4
CONV_WIDTH = D_MODEL // 2
CONV_KERNEL = 3
D_FF = 5632
NORM_EPS = 1e-5
IN_SPLIT_SIZES = (ATTN_WIDTH,) * 3 + (CONV_WIDTH,) * 3 + (D_MODEL,) * 2
IN_COLS = sum(IN_SPLIT_SIZES)

kernel_name = "hybrid_gated_conv_dilated_attn_macaron"


def rmsnorm(x, g):
    xf = x.astype(jnp.float32)
    y = xf * lax.rsqrt(jnp.mean(xf * xf, axis=-1, keepdims=True) + NORM_EPS)
    return (y * g.astype(jnp.float32)).astype(x.dtype)


def swiglu(h, w_in, w_out):
    gu = h @ w_in
    gate, up = jnp.split(gu, 2, axis=-1)
    return (jax.nn.silu(gate) * up) @ w_out


def partial_rotary(x, positions):
    half = ROPE_DIM // 2
    inv_freq = ROPE_THETA ** (-(jnp.arange(half, dtype=jnp.float32) * 2.0) / ROPE_DIM)
    ang = positions.astype(jnp.float32)[:, None] * inv_freq[None, :]
    cos = jnp.cos(ang)[None, :, None, :]
    sin = jnp.sin(ang)[None, :, None, :]
    xr = x[..., :ROPE_DIM].astype(jnp.float32)
    x1, x2 = xr[..., :half], xr[..., half:]
    rot = jnp.concatenate([x1 * cos - x2 * sin, x2 * cos + x1 * sin], axis=-1)
    return jnp.concatenate([rot.astype(x.dtype), x[..., ROPE_DIM:]], axis=-1)


def dilated_window_attention(q, k, v, window, dilation):
    B, S, H, Dh = q.shape
    span = window // dilation
    assert span <= ATTN_BLOCK
    unit = dilation * ATTN_BLOCK
    s_pad = -(-S // unit) * unit
    L = s_pad // dilation
    nblk = L // ATTN_BLOCK

    def to_strided(t):
        t = jnp.pad(t, ((0, 0), (0, s_pad - S), (0, 0), (0, 0)))
        t = t.reshape(B, L, dilation, H, Dh).transpose(0, 2, 3, 1, 4)
        return t.reshape(B, dilation, H, nblk, ATTN_BLOCK, Dh)

    def with_prev(t):
        prev = jnp.pad(t, ((0, 0), (0, 0), (0, 0), (1, 0), (0, 0), (0, 0)))[:, :, :, :-1]
        return jnp.concatenate([prev, t], axis=4)

    qs = to_strided(q)
    kk = with_prev(to_strided(k))
    vv = with_prev(to_strided(v))

    scores = jnp.einsum('bdhnqe,bdhnke->bdhnqk', qs, kk).astype(jnp.float32) * (Dh ** -0.5)
    qi = jnp.arange(ATTN_BLOCK)[:, None]
    kj = jnp.arange(2 * ATTN_BLOCK)[None, :]
    dist = ATTN_BLOCK + qi - kj
    band = (dist >= 0) & (dist <= span)
    blk = jnp.arange(nblk)[:, None, None]
    mask = band[None] & ((blk > 0) | (kj >= ATTN_BLOCK)[None])
    scores = jnp.where(mask, scores, -jnp.inf)
    m = jnp.max(scores, axis=-1, keepdims=True)
    p = jnp.exp(scores - m)
    l = jnp.sum(p, axis=-1, keepdims=True)
    o = jnp.einsum('bdhnqk,bdhnke->bdhnqe', p.astype(vv.dtype), vv).astype(jnp.float32) / l
    lse = m[..., 0] + jnp.log(l[..., 0])

    o = o.reshape(B, dilation, H, L, Dh).transpose(0, 3, 1, 2, 4).reshape(B, s_pad, H, Dh)[:, :S]
    lse = lse.reshape(B, dilation, H, L).transpose(0, 3, 1, 2).reshape(B, s_pad, H)[:, :S]
    return o, lse


def short_conv_branch(xc, b_gate, c_gate, w_conv):
    u = c_gate * xc
    kern = w_conv.reshape(CONV_KERNEL, 1, CONV_WIDTH).astype(u.dtype)
    conv = lax.conv_general_dilated(
        u, kern, window_strides=(1,), padding=[(CONV_KERNEL - 1, 0)],
        dimension_numbers=('NWC', 'WIO', 'NWC'), feature_group_count=CONV_WIDTH)
    return b_gate * conv


def dilated_attention_branch(q, k, v, positions):
    B, S, _ = q.shape
    q = partial_rotary(q.reshape(B, S, N_ATTN_HEADS, HEAD_DIM), positions)
    k = partial_rotary(k.reshape(B, S, N_ATTN_HEADS, HEAD_DIM), positions)
    v = v.reshape(B, S, N_ATTN_HEADS, HEAD_DIM)
    outs, lses = [], []
    for g, (window, dilation) in enumerate(ATTN_PATTERNS):
        hs = slice(g * HEADS_PER_GROUP, (g + 1) * HEADS_PER_GROUP)
        o, lse = dilated_window_attention(q[:, :, hs], k[:, :, hs], v[:, :, hs], window, dilation)
        outs.append(o)
        lses.append(lse)
    wts = jax.nn.softmax(jnp.stack(lses, axis=0), axis=0)
    o = jnp.einsum('gbsh,gbshe->bshe', wts, jnp.stack(outs, axis=0))
    return o.reshape(B, S, ATTN_OUT_WIDTH).astype(q.dtype)


def token_mixer(h, positions, w_in, w_conv, w_conv_out, w_attn_out, w_o):
    proj = h @ w_in
    splits = [int(i) for i in np.cumsum(IN_SPLIT_SIZES)[:-1]]
    q, k, v, xc, b_gate, c_gate, g_conv, g_attn = jnp.split(proj, splits, axis=-1)
    y_conv = short_conv_branch(xc, b_gate, c_gate, w_conv) @ w_conv_out
    y_attn = dilated_attention_branch(q, k, v, positions) @ w_attn_out
    merged = jax.nn.sigmoid(g_conv) * y_conv + jax.nn.sigmoid(g_attn) * y_attn
    return merged @ w_o


def setup_inputs(seed: int = 0) -> dict:
    key = jax.random.key(seed)
    ks = jax.random.split(key, 16)
    f32 = jnp.float32

    def w(k, shape, fan_in, scale=1.0):
        return jax.random.normal(k, shape, f32) * (scale * fan_in ** -0.5)

    def gain(k, shape):
        return jnp.ones(shape, f32) + 0.02 * jax.random.normal(k, shape, f32)

    return {
        "x": jax.random.normal(ks[0], (BATCH, SEQ, D_MODEL), f32),
        "ffn1_norm": gain(ks[1], (DEPTH, D_MODEL)),
        "w_ffn1_in": w(ks[2], (DEPTH, D_MODEL, 2 * D_FF), D_MODEL),
        "w_ffn1_out": w(ks[3], (DEPTH, D_FF, D_MODEL), D_FF),
        "mix_norm": gain(ks[4], (DEPTH, D_MODEL)),
        "w_in": w(ks[5], (DEPTH, D_MODEL, IN_COLS), D_MODEL),
        "w_conv": w(ks[6], (DEPTH, CONV_KERNEL, CONV_WIDTH), CONV_KERNEL),
        "w_conv_out": w(ks[7], (DEPTH, CONV_WIDTH, D_MODEL), CONV_WIDTH),
        "w_attn_out": w(ks[8], (DEPTH, ATTN_OUT_WIDTH, D_MODEL), ATTN_OUT_WIDTH),
        "w_o": w(ks[9], (DEPTH, D_MODEL, D_MODEL), D_MODEL),
        "ffn2_norm": gain(ks[10], (DEPTH, D_MODEL)),
        "w_ffn2_in": w(ks[11], (DEPTH, D_MODEL, 2 * D_FF), D_MODEL),
        "w_ffn2_out": w(ks[12], (DEPTH, D_FF, D_MODEL), D_FF),
        "final_norm": gain(ks[13], (D_MODEL,)),
    }


def reference(x, ffn1_norm, w_ffn1_in, w_ffn1_out, mix_norm, w_in, w_conv, w_conv_out,
              w_attn_out, w_o, ffn2_norm, w_ffn2_in, w_ffn2_out, final_norm):
    S = x.shape[1]
    positions = jnp.arange(S, dtype=jnp.int32)
    for l in range(DEPTH):
        x = x + 0.5 * swiglu(rmsnorm(x, ffn1_norm[l]), w_ffn1_in[l], w_ffn1_out[l])
        h = rmsnorm(x, mix_norm[l])
        x = x + token_mixer(h, positions, w_in[l], w_conv[l], w_conv_out[l], w_attn_out[l], w_o[l])
        x = x + 0.5 * swiglu(rmsnorm(x, ffn2_norm[l]), w_ffn2_in[l], w_ffn2_out[l])
    return rmsnorm(x, final_norm)
```

```python
import functools

import jax
import jax.numpy as jnp
from jax import lax
from jax.experimental import pallas as pl
from jax.experimental.pallas import tpu as pltpu

D_MODEL = 2048
HEAD_DIM = 128
HEADS_PER_GROUP = 4
ATTN_PATTERNS = ((128, 1), (512, 4), (2048, 16))
N_GROUPS = len(ATTN_PATTERNS)
ATTN_WIDTH = N_GROUPS * HEADS_PER_GROUP * HEAD_DIM
GROUP_WIDTH = HEADS_PER_GROUP * HEAD_DIM
ATTN_BLOCK = 128
ROPE_THETA = 500000.0
ROPE_DIM = HEAD_DIM // 4
CONV_WIDTH = D_MODEL // 2
CONV_KERNEL = 3
D_FF = 5632
NORM_EPS = 1e-5
QKV_COLS = 3 * ATTN_WIDTH
XBC_COLS = 3 * CONV_WIDTH

V7X_VMEM_LIMIT_BYTES = 58 * 1024 * 1024
LSE_LANES = 128
LSE_LANES_PER_HEAD = LSE_LANES // HEADS_PER_GROUP
CONV_HALO_ROWS = 8

BF16 = jnp.bfloat16
F32 = jnp.float32


def _rmsnorm_f32(x, g):
    return (x * lax.rsqrt(jnp.mean(x * x, axis=-1, keepdims=True) + NORM_EPS)) * g


def _compiler_params(n_axes):
    return pltpu.CompilerParams(
        dimension_semantics=("arbitrary",) * n_axes,
        vmem_limit_bytes=V7X_VMEM_LIMIT_BYTES,
    )


def _resident(shape):
    return pl.BlockSpec(shape, lambda *_: (0,) * len(shape), pipeline_mode=pl.Buffered(1))


def _ffn_kernel(x_ref, g_ref, wg_ref, wu_ref, wo_ref, gf_ref, out_ref, h_ref, *, final_norm):
    j = pl.program_id(1)

    @pl.when(j == 0)
    def _():
        x = x_ref[...]
        h_ref[...] = _rmsnorm_f32(x, g_ref[...]).astype(BF16)
        out_ref[...] = x

    h = h_ref[...]
    gate = jnp.dot(h, wg_ref[...], preferred_element_type=F32)
    up = jnp.dot(h, wu_ref[...], preferred_element_type=F32)
    a = (0.5 * (gate * jax.nn.sigmoid(gate)) * up).astype(BF16)
    out_ref[...] += jnp.dot(a, wo_ref[...], preferred_element_type=F32)

    if final_norm:

        @pl.when(j == pl.num_programs(1) - 1)
        def _():
            out_ref[...] = _rmsnorm_f32(out_ref[...], gf_ref[...])


def _ffn(x, gain, w_in, w_out, final_gain, *, final_norm, tm, tf):
    t = x.shape[0]
    nf = D_FF // tf
    return pl.pallas_call(
        functools.partial(_ffn_kernel, final_norm=final_norm),
        grid=(t // tm, nf),
        in_specs=[
            pl.BlockSpec((tm, D_MODEL), lambda i, j: (i, 0)),
            pl.BlockSpec((1, D_MODEL), lambda i, j: (0, 0)),
            pl.BlockSpec((D_MODEL, tf), lambda i, j: (0, j)),
            pl.BlockSpec((D_MODEL, tf), lambda i, j: (0, j + nf)),
            pl.BlockSpec((tf, D_MODEL), lambda i, j: (j, 0)),
            pl.BlockSpec((1, D_MODEL), lambda i, j: (0, 0)),
        ],
        out_specs=pl.BlockSpec((tm, D_MODEL), lambda i, j: (i, 0)),
        out_shape=jax.ShapeDtypeStruct((t, D_MODEL), F32),
        scratch_shapes=[pltpu.VMEM((tm, D_MODEL), BF16)],
        compiler_params=_compiler_params(2),
        name="ffn_final" if final_norm else "ffn",
    )(x, gain, w_in, w_in, w_out, final_gain)


def _qkv_kernel(x_ref, g_ref, w_ref, cos_ref, sin_ref, out_ref, h_ref, *, n_rot_tiles):
    j = pl.program_id(1)

    @pl.when(j == 0)
    def _():
        h_ref[...] = _rmsnorm_f32(x_ref[...], g_ref[...]).astype(BF16)

    acc = jnp.dot(h_ref[...], w_ref[...], preferred_element_type=F32)

    @pl.when(j < n_rot_tiles)
    def _():
        c = cos_ref[...]
        s = sin_ref[...]
        lane = lax.broadcasted_iota(jnp.int32, c.shape, 1)
        first_half = lane < ROPE_DIM // 2
        for hh in range(acc.shape[1] // HEAD_DIM):
            xh = acc[:, hh * HEAD_DIM:(hh + 1) * HEAD_DIM]
            nxt = pltpu.roll(xh, HEAD_DIM - ROPE_DIM // 2, axis=1)
            prv = pltpu.roll(xh, ROPE_DIM // 2, axis=1)
            rot = xh * c + jnp.where(first_half, nxt, prv) * s
            out_ref[:, hh * HEAD_DIM:(hh + 1) * HEAD_DIM] = rot.astype(BF16)

    @pl.when(j >= n_rot_tiles)
    def _():
        out_ref[...] = acc.astype(BF16)


def _qkv(x1, gain, w_qkv, cos_tab, sin_tab, *, seq, tm, tn):
    t = x1.shape[0]
    tiles_per_seq = seq // tm
    return pl.pallas_call(
        functools.partial(_qkv_kernel, n_rot_tiles=2 * ATTN_WIDTH // tn),
        grid=(t // tm, QKV_COLS // tn),
        in_specs=[
            pl.BlockSpec((tm, D_MODEL), lambda i, j: (i, 0)),
            pl.BlockSpec((1, D_MODEL), lambda i, j: (0, 0)),
            pl.BlockSpec((D_MODEL, tn), lambda i, j: (0, j)),
            pl.BlockSpec((tm, HEAD_DIM), lambda i, j: (i % tiles_per_seq, 0)),
            pl.BlockSpec((tm, HEAD_DIM), lambda i, j: (i % tiles_per_seq, 0)),
        ],
        out_specs=pl.BlockSpec((tm, tn), lambda i, j: (i, j)),
        out_shape=jax.ShapeDtypeStruct((t, QKV_COLS), BF16),
        scratch_shapes=[pltpu.VMEM((tm, D_MODEL), BF16)],
        compiler_params=_compiler_params(2),
        name="qkv_rotary",
    )(x1, gain, w_qkv, cos_tab, sin_tab)


def _attn_kernel(q_ref, kc_ref, kp_ref, vc_ref, vp_ref, o_ref, lse_ref):
    n = pl.program_id(2)
    tq = q_ref.shape[1]
    scale = HEAD_DIM ** -0.5
    qi = lax.broadcasted_iota(jnp.int32, (ATTN_BLOCK, ATTN_BLOCK), 0)
    kj = lax.broadcasted_iota(jnp.int32, (ATTN_BLOCK, ATTN_BLOCK), 1)
    cur_mask = kj <= qi
    prev_mask = kj >= qi
    first_prev_mask = jnp.logical_and(prev_mask, n > 0)
    lane = lax.broadcasted_iota(jnp.int32, (ATTN_BLOCK, LSE_LANES), 1)
    nt = (((1,), (1,)), ((), ()))

    for qb in range(tq // ATTN_BLOCK):
        rows = slice(qb * ATTN_BLOCK, (qb + 1) * ATTN_BLOCK)
        prows = slice((qb - 1) * ATTN_BLOCK, qb * ATTN_BLOCK)
        lse_tile = jnp.zeros((ATTN_BLOCK, LSE_LANES), F32)
        for hh in range(HEADS_PER_GROUP):
            cols = slice(hh * HEAD_DIM, (hh + 1) * HEAD_DIM)
            q = q_ref[0, rows, cols]
            kc = kc_ref[0, rows, cols]
            vc = vc_ref[0, rows, cols]
            if qb == 0:
                kp = kp_ref[0, :, cols]
                vp = vp_ref[0, :, cols]
                pmask = first_prev_mask
            else:
                kp = kc_ref[0, prows, cols]
                vp = vc_ref[0, prows, cols]
                pmask = prev_mask
            s_c = lax.dot_general(q, kc, nt, preferred_element_type=F32) * scale
            s_p = lax.dot_general(q, kp, nt, preferred_element_type=F32) * scale
            s_c = jnp.where(cur_mask, s_c, -jnp.inf)
            s_p = jnp.where(pmask, s_p, -jnp.inf)
            m = jnp.maximum(jnp.max(s_c, axis=-1, keepdims=True),
                            jnp.max(s_p, axis=-1, keepdims=True))
            p_c = jnp.exp(s_c - m)
            p_p = jnp.exp(s_p - m)
            l = jnp.sum(p_c, axis=-1, keepdims=True) + jnp.sum(p_p, axis=-1, keepdims=True)
            pv = (jnp.dot(p_c.astype(BF16), vc, preferred_element_type=F32)
                  + jnp.dot(p_p.astype(BF16), vp, preferred_element_type=F32))
            o_ref[0, rows, cols] = (pv / l).astype(BF16)
            lse = m + jnp.log(l)
            lse_tile = jnp.where(lane // LSE_LANES_PER_HEAD == hh, lse, lse_tile)
        lse_ref[0, rows, :] = lse_tile


def _attn_group(qkv, *, batch, seq, group, dilation, tq):
    strided_len = seq // dilation
    tq = min(tq, strided_len)
    qkv_view = qkv.reshape(batch, strided_len, dilation * QKV_COLS)
    col_blocks = QKV_COLS // GROUP_WIDTH
    sub = tq // ATTN_BLOCK

    def cur(offset):
        return pl.BlockSpec((1, tq, GROUP_WIDTH),
                            lambda b, r, n: (b, n, r * col_blocks + offset + group))

    def prev(offset):
        return pl.BlockSpec((1, ATTN_BLOCK, GROUP_WIDTH),
                            lambda b, r, n: (b, jnp.maximum(n * sub - 1, 0),
                                             r * col_blocks + offset + group))

    o, lse = pl.pallas_call(
        _attn_kernel,
        grid=(batch, dilation, strided_len // tq),
        in_specs=[cur(0), cur(N_GROUPS), prev(N_GROUPS), cur(2 * N_GROUPS), prev(2 * N_GROUPS)],
        out_specs=[
            pl.BlockSpec((1, tq, GROUP_WIDTH), lambda b, r, n: (b, n, r)),
            pl.BlockSpec((1, tq, LSE_LANES), lambda b, r, n: (b, n, r)),
        ],
        out_shape=[
            jax.ShapeDtypeStruct((batch, strided_len, dilation * GROUP_WIDTH), BF16),
            jax.ShapeDtypeStruct((batch, strided_len, dilation * LSE_LANES), F32),
        ],
        compiler_params=_compiler_params(3),
        name=f"dilated_attn_d{dilation}",
    )(qkv_view, qkv_view, qkv_view, qkv_view, qkv_view)
    return (o.reshape(batch * seq, GROUP_WIDTH), lse.reshape(batch * seq, LSE_LANES))


def _conv_kernel(x_ref, g_ref, wxbc_ref, wgc_ref, wco_ref, wconv_ref, out_ref,
                 ubuf_ref, yc_ref, *, tiles_per_seq, chunk):
    i = pl.program_id(0)
    tm = x_ref.shape[0]
    halo = CONV_HALO_ROWS

    @pl.when(i % tiles_per_seq == 0)
    def _():
        ubuf_ref[0:halo, :] = jnp.zeros((halo, CONV_WIDTH), F32)

    h = _rmsnorm_f32(x_ref[...], g_ref[...]).astype(BF16)
    for c in range(CONV_WIDTH // chunk):
        cols = slice(c * chunk, (c + 1) * chunk)
        xc = jnp.dot(h, wxbc_ref[:, cols], preferred_element_type=F32)
        cg = jnp.dot(h, wxbc_ref[:, 2 * CONV_WIDTH + c * chunk:2 * CONV_WIDTH + (c + 1) * chunk],
                     preferred_element_type=F32)
        u = cg * xc
        ubuf_ref[halo:halo + tm, cols] = u
        conv = (wconv_ref[2:3, cols] * u
                + wconv_ref[1:2, cols] * ubuf_ref[halo - 1:halo - 1 + tm, cols]
                + wconv_ref[0:1, cols] * ubuf_ref[halo - 2:halo - 2 + tm, cols])
        ubuf_ref[0:halo, cols] = ubuf_ref[tm:tm + halo, cols]
        bg = jnp.dot(h, wxbc_ref[:, CONV_WIDTH + c * chunk:CONV_WIDTH + (c + 1) * chunk],
                     preferred_element_type=F32)
        yb = (bg * conv).astype(BF16)
        part = jnp.dot(yb, wco_ref[cols, :], preferred_element_type=F32)
        if c == 0:
            yc_ref[...] = part
        else:
            yc_ref[...] += part
    for c in range(D_MODEL // chunk):
        cols = slice(c * chunk, (c + 1) * chunk)
        gate = jnp.dot(h, wgc_ref[:, cols], preferred_element_type=F32)
        out_ref[:, cols] = (jax.nn.sigmoid(gate) * yc_ref[:, cols]).astype(BF16)


def _conv_branch(x1, gain, w_xbc, w_gc, w_co, w_conv, *, seq, tm, chunk):
    t = x1.shape[0]
    return pl.pallas_call(
        functools.partial(_conv_kernel, tiles_per_seq=seq // tm, chunk=chunk),
        grid=(t // tm,),
        in_specs=[
            pl.BlockSpec((tm, D_MODEL), lambda i: (i, 0)),
            _resident((1, D_MODEL)),
            _resident((D_MODEL, XBC_COLS)),
            _resident((D_MODEL, D_MODEL)),
            _resident((CONV_WIDTH, D_MODEL)),
            _resident((CONV_KERNEL, CONV_WIDTH)),
        ],
        out_specs=pl.BlockSpec((tm, D_MODEL), lambda i: (i, 0)),
        out_shape=jax.ShapeDtypeStruct((t, D_MODEL), BF16),
        scratch_shapes=[
            pltpu.VMEM((tm + CONV_HALO_ROWS, CONV_WIDTH), F32),
            pltpu.VMEM((tm, D_MODEL), F32),
        ],
        compiler_params=_compiler_params(1),
        name="gated_short_conv",
    )(x1, gain, w_xbc, w_gc, w_co, w_conv)


def _merge_kernel(x_ref, g_ref, conv_ref, o0_ref, o1_ref, o2_ref, l0_ref, l1_ref, l2_ref,
                  wga_ref, wao_ref, wo_ref, out_ref, oc_ref, merged_ref, *, chunk):
    x = x_ref[...]
    h = _rmsnorm_f32(x, g_ref[...]).astype(BF16)

    lses = (l0_ref[...], l1_ref[...], l2_ref[...])
    m = jnp.maximum(jnp.maximum(lses[0], lses[1]), lses[2])
    es = [jnp.exp(l - m) for l in lses]
    den = es[0] + es[1] + es[2]
    ws = [e / den for e in es]
    o_refs = (o0_ref, o1_ref, o2_ref)
    for hh in range(HEADS_PER_GROUP):
        cols = slice(hh * HEAD_DIM, (hh + 1) * HEAD_DIM)
        lane0 = hh * LSE_LANES_PER_HEAD
        acc = None
        for g in range(N_GROUPS):
            term = ws[g][:, lane0:lane0 + 1] * o_refs[g][:, cols].astype(F32)
            acc = term if acc is None else acc + term
        oc_ref[:, cols] = acc.astype(BF16)

    ya = jnp.dot(oc_ref[...], wao_ref[...], preferred_element_type=F32)
    for c in range(D_MODEL // chunk):
        cols = slice(c * chunk, (c + 1) * chunk)
        gate = jnp.dot(h, wga_ref[:, cols], preferred_element_type=F32)
        merged = conv_ref[:, cols].astype(F32) + jax.nn.sigmoid(gate) * ya[:, cols]
        merged_ref[:, cols] = merged.astype(BF16)
    out_ref[...] = x + jnp.dot(merged_ref[...], wo_ref[...], preferred_element_type=F32)


def _merge(x1, gain, conv, outs, lses, w_ga, w_ao, w_o, *, tm, chunk):
    t = x1.shape[0]
    row = lambda width: pl.BlockSpec((tm, width), lambda i: (i, 0))
    return pl.pallas_call(
        functools.partial(_merge_kernel, chunk=chunk),
        grid=(t // tm,),
        in_specs=[
            row(D_MODEL), _resident((1, D_MODEL)), row(D_MODEL),
            row(GROUP_WIDTH), row(GROUP_WIDTH), row(GROUP_WIDTH),
            row(LSE_LANES), row(LSE_LANES), row(LSE_LANES),
            _resident((D_MODEL, D_MODEL)),
            _resident((GROUP_WIDTH, D_MODEL)),
            _resident((D_MODEL, D_MODEL)),
        ],
        out_specs=row(D_MODEL),
        out_shape=jax.ShapeDtypeStruct((t, D_MODEL), F32),
        scratch_shapes=[
            pltpu.VMEM((tm, GROUP_WIDTH), BF16),
            pltpu.VMEM((tm, D_MODEL), BF16),
        ],
        compiler_params=_compiler_params(1),
        name="gated_merge",
    )(x1, gain, conv, *outs, *lses, w_ga, w_ao, w_o)


def _rotary_tables(seq):
    half = ROPE_DIM // 2
    inv_freq = ROPE_THETA ** (-(jnp.arange(half, dtype=F32) * 2.0) / ROPE_DIM)
    ang = jnp.arange(seq, dtype=jnp.int32).astype(F32)[:, None] * inv_freq[None, :]
    cos, sin = jnp.cos(ang), jnp.sin(ang)
    pad = HEAD_DIM - ROPE_DIM
    cos_tab = jnp.concatenate([cos, cos, jnp.ones((seq, pad), F32)], axis=-1)
    sin_tab = jnp.concatenate([-sin, sin, jnp.zeros((seq, pad), F32)], axis=-1)
    return cos_tab, sin_tab


@jax.jit
def kernel(x, ffn1_norm, w_ffn1_in, w_ffn1_out, mix_norm, w_in, w_conv, w_conv_out,
           w_attn_out, w_o, ffn2_norm, w_ffn2_in, w_ffn2_out, final_norm):
    batch, seq, d = x.shape
    depth = w_in.shape[0]
    xt = x.reshape(batch * seq, d)
    cos_tab, sin_tab = _rotary_tables(seq)
    final_gain = final_norm.reshape(1, d)
    q_end, xbc_end, gc_end = QKV_COLS, QKV_COLS + XBC_COLS, QKV_COLS + XBC_COLS + D_MODEL
    for l in range(depth):
        xt = _ffn(xt, ffn1_norm[l].reshape(1, d), w_ffn1_in[l].astype(BF16),
                  w_ffn1_out[l].astype(BF16), final_gain, final_norm=False, tm=512, tf=512)
        mix_gain = mix_norm[l].reshape(1, d)
        w_in_l = w_in[l]
        qkv = _qkv(xt, mix_gain, w_in_l[:, :q_end].astype(BF16), cos_tab, sin_tab,
                   seq=seq, tm=512, tn=GROUP_WIDTH)
        outs, lses = [], []
        for g, (window, dilation) in enumerate(ATTN_PATTERNS):
            assert window // dilation == ATTN_BLOCK
            o, lse = _attn_group(qkv, batch=batch, seq=seq, group=g, dilation=dilation, tq=512)
            outs.append(o)
            lses.append(lse)
        conv = _conv_branch(xt, mix_gain, w_in_l[:, q_end:xbc_end].astype(BF16),
                            w_in_l[:, xbc_end:gc_end].astype(BF16), w_conv_out[l].astype(BF16),
                            w_conv[l], seq=seq, tm=512, chunk=256)
        xt = _merge(xt, mix_gain, conv, outs, lses, w_in_l[:, gc_end:].astype(BF16),
                    w_attn_out[l].astype(BF16), w_o[l].astype(BF16), tm=256, chunk=512)
        xt = _ffn(xt, ffn2_norm[l].reshape(1, d), w_ffn2_in[l].astype(BF16),
                  w_ffn2_out[l].astype(BF16), final_gain, final_norm=(l == depth - 1),
                  tm=512, tf=512)
    return xt.reshape(batch, seq, d)
```

```python
import functools

import jax
import jax.numpy as jnp
from jax import lax
from jax.experimental import pallas as pl
from jax.experimental.pallas import tpu as pltpu

D_MODEL = 2048
HEAD_DIM = 128
HEADS_PER_GROUP = 4
ATTN_PATTERNS = ((128, 1), (512, 4), (2048, 16))
N_GROUPS = len(ATTN_PATTERNS)
ATTN_WIDTH = N_GROUPS * HEADS_PER_GROUP * HEAD_DIM
GROUP_WIDTH = HEADS_PER_GROUP * HEAD_DIM
ATTN_BLOCK = 128
ROPE_THETA = 500000.0
ROPE_DIM = HEAD_DIM // 4
CONV_WIDTH = D_MODEL // 2
CONV_KERNEL = 3
D_FF = 5632
NORM_EPS = 1e-5
QKV_COLS = 3 * ATTN_WIDTH
XBC_COLS = 3 * CONV_WIDTH

V7X_VMEM_LIMIT_BYTES = 58 * 1024 * 1024
LSE_LANES = 128
LSE_LANES_PER_HEAD = LSE_LANES // HEADS_PER_GROUP
CONV_HALO_ROWS = 8

BF16 = jnp.bfloat16
F32 = jnp.float32


def _rmsnorm_f32(x, g):
    return (x * lax.rsqrt(jnp.mean(x * x, axis=-1, keepdims=True) + NORM_EPS)) * g


def _compiler_params(n_axes):
    return pltpu.CompilerParams(
        dimension_semantics=("arbitrary",) * n_axes,
        vmem_limit_bytes=V7X_VMEM_LIMIT_BYTES,
    )


def _resident(shape):
    return pl.BlockSpec(shape, lambda *_: (0,) * len(shape), pipeline_mode=pl.Buffered(1))


def _ffn_kernel(x_ref, g_ref, wg_ref, wu_ref, wo_ref, gf_ref, out_ref, h_ref, *, final_norm):
    j = pl.program_id(1)

    @pl.when(j == 0)
    def _():
        x = x_ref[...]
        h_ref[...] = _rmsnorm_f32(x, g_ref[...]).astype(BF16)
        out_ref[...] = x

    h = h_ref[...]
    gate = jnp.dot(h, wg_ref[...], preferred_element_type=F32)
    up = jnp.dot(h, wu_ref[...], preferred_element_type=F32)
    a = (0.5 * (gate * jax.nn.sigmoid(gate)) * up).astype(BF16)
    out_ref[...] += jnp.dot(a, wo_ref[...], preferred_element_type=F32)

    if final_norm:

        @pl.when(j == pl.num_programs(1) - 1)
        def _():
            out_ref[...] = _rmsnorm_f32(out_ref[...], gf_ref[...])


def _ffn(x, gain, w_in, w_out, final_gain, *, final_norm, tm, tf):
    t = x.shape[0]
    nf = D_FF // tf
    return pl.pallas_call(
        functools.partial(_ffn_kernel, final_norm=final_norm),
        grid=(t // tm, nf),
        in_specs=[
            pl.BlockSpec((tm, D_MODEL), lambda i, j: (i, 0)),
            pl.BlockSpec((1, D_MODEL), lambda i, j: (0, 0)),
            pl.BlockSpec((D_MODEL, tf), lambda i, j: (0, j)),
            pl.BlockSpec((D_MODEL, tf), lambda i, j: (0, j + nf)),
            pl.BlockSpec((tf, D_MODEL), lambda i, j: (j, 0)),
            pl.BlockSpec((1, D_MODEL), lambda i, j: (0, 0)),
        ],
        out_specs=pl.BlockSpec((tm, D_MODEL), lambda i, j: (i, 0)),
        out_shape=jax.ShapeDtypeStruct((t, D_MODEL), F32),
        scratch_shapes=[pltpu.VMEM((tm, D_MODEL), BF16)],
        compiler_params=_compiler_params(2),
        name="ffn_final" if final_norm else "ffn",
    )(x, gain, w_in, w_in, w_out, final_gain)


def _qkv_kernel(x_ref, g_ref, w_ref, cos_ref, sin_ref, *refs):
    out_refs, slab_ref = refs[:N_GROUPS], refs[N_GROUPS]
    tm = x_ref.shape[0]
    h = _rmsnorm_f32(x_ref[...], g_ref[...]).astype(BF16)
    c = cos_ref[...]
    s = sin_ref[...]
    first_half = lax.broadcasted_iota(jnp.int32, c.shape, 1) < ROPE_DIM // 2
    slab = 0
    for kind in range(3):
        for g, (_, dilation) in enumerate(ATTN_PATTERNS):
            col0 = kind * ATTN_WIDTH + g * GROUP_WIDTH
            acc = jnp.dot(h, w_ref[:, col0:col0 + GROUP_WIDTH], preferred_element_type=F32)
            for hh in range(HEADS_PER_GROUP):
                xh = acc[:, hh * HEAD_DIM:(hh + 1) * HEAD_DIM]
                if kind < 2:
                    nxt = pltpu.roll(xh, HEAD_DIM - ROPE_DIM // 2, axis=1)
                    prv = pltpu.roll(xh, ROPE_DIM // 2, axis=1)
                    xh = xh * c + jnp.where(first_half, nxt, prv) * s
                cols = slice(kind * GROUP_WIDTH + hh * HEAD_DIM, kind * GROUP_WIDTH + (hh + 1) * HEAD_DIM)
                if dilation == 1:
                    out_refs[g][0, :, cols] = xh.astype(BF16)
                else:
                    slab_ref[slab] = xh
                    for r in range(dilation):
                        rows = slab_ref[slab, pl.ds(r, tm // dilation, stride=dilation), :]
                        out_refs[g][r, :, cols] = rows.astype(BF16)
                    slab += 1


def _qkv(x1, gain, w_qkv, cos_tab, sin_tab, *, batch, seq, tm):
    t = x1.shape[0]
    tiles_per_seq = seq // tm
    n_slabs = 3 * HEADS_PER_GROUP * sum(1 for _, d in ATTN_PATTERNS if d > 1)
    return pl.pallas_call(
        _qkv_kernel,
        grid=(t // tm,),
        in_specs=[
            pl.BlockSpec((tm, D_MODEL), lambda i: (i, 0)),
            _resident((1, D_MODEL)),
            _resident((D_MODEL, QKV_COLS)),
            pl.BlockSpec((tm, HEAD_DIM), lambda i: (i % tiles_per_seq, 0)),
            pl.BlockSpec((tm, HEAD_DIM), lambda i: (i % tiles_per_seq, 0)),
        ],
        out_specs=[
            pl.BlockSpec((None, d, tm // d, 3 * GROUP_WIDTH),
                         lambda i: (i // tiles_per_seq, 0, i % tiles_per_seq, 0))
            for _, d in ATTN_PATTERNS
        ],
        out_shape=[
            jax.ShapeDtypeStruct((batch, d, seq // d, 3 * GROUP_WIDTH), BF16)
            for _, d in ATTN_PATTERNS
        ],
        scratch_shapes=[pltpu.VMEM((n_slabs, tm, HEAD_DIM), F32)],
        compiler_params=_compiler_params(1),
        name="qkv_rotary",
    )(x1, gain, w_qkv, cos_tab, sin_tab)


def _attn_kernel(q_ref, kc_ref, kp_ref, vc_ref, vp_ref, o_ref, lse_ref):
    n = pl.program_id(2)
    tq = q_ref.shape[0]
    scale = HEAD_DIM ** -0.5
    qi = lax.broadcasted_iota(jnp.int32, (ATTN_BLOCK, 2 * ATTN_BLOCK), 0)
    kj = lax.broadcasted_iota(jnp.int32, (ATTN_BLOCK, 2 * ATTN_BLOCK), 1)
    band = jnp.logical_and(kj >= qi, kj <= qi + ATTN_BLOCK)
    first_band = jnp.logical_and(band, jnp.logical_or(kj >= ATTN_BLOCK, n > 0))
    lane = lax.broadcasted_iota(jnp.int32, (ATTN_BLOCK, LSE_LANES), 1)
    nt = (((1,), (1,)), ((), ()))

    for qb in range(tq // ATTN_BLOCK):
        rows = slice(qb * ATTN_BLOCK, (qb + 1) * ATTN_BLOCK)
        window = slice((qb - 1) * ATTN_BLOCK, (qb + 1) * ATTN_BLOCK)
        lse_tile = jnp.zeros((ATTN_BLOCK, LSE_LANES), F32)
        for hh in range(HEADS_PER_GROUP):
            cols = slice(hh * HEAD_DIM, (hh + 1) * HEAD_DIM)
            q = q_ref[rows, cols]
            if qb == 0:
                k2 = jnp.concatenate([kp_ref[:, cols], kc_ref[rows, cols]], axis=0)
                v2 = jnp.concatenate([vp_ref[:, cols], vc_ref[rows, cols]], axis=0)
                mask = first_band
            else:
                k2 = kc_ref[window, cols]
                v2 = vc_ref[window, cols]
                mask = band
            s = lax.dot_general(q, k2, nt, preferred_element_type=F32) * scale
            s = jnp.where(mask, s, -jnp.inf)
            m = jnp.max(s, axis=-1, keepdims=True)
            p = jnp.exp(s - m)
            l = jnp.sum(p, axis=-1, keepdims=True)
            pv = jnp.dot(p.astype(BF16), v2, preferred_element_type=F32)
            o_ref[rows, cols] = (pv / l).astype(BF16)
            lse_tile = jnp.where(lane // LSE_LANES_PER_HEAD == hh, m + jnp.log(l), lse_tile)
        lse_ref[rows, :] = lse_tile


def _attn_group(qkv_g, *, dilation, tq):
    batch, _, strided_len, _ = qkv_g.shape
    tq = min(tq, strided_len)
    sub = tq // ATTN_BLOCK

    def cur(kind):
        return pl.BlockSpec((None, None, tq, GROUP_WIDTH), lambda b, r, n: (b, r, n, kind))

    def prev(kind):
        return pl.BlockSpec((None, None, ATTN_BLOCK, GROUP_WIDTH),
                            lambda b, r, n: (b, r, jnp.maximum(n * sub - 1, 0), kind))

    return pl.pallas_call(
        _attn_kernel,
        grid=(batch, dilation, strided_len // tq),
        in_specs=[cur(0), cur(1), prev(1), cur(2), prev(2)],
        out_specs=[
            pl.BlockSpec((None, None, tq, GROUP_WIDTH), lambda b, r, n: (b, r, n, 0)),
            pl.BlockSpec((None, None, tq, LSE_LANES), lambda b, r, n: (b, r, n, 0)),
        ],
        out_shape=[
            jax.ShapeDtypeStruct((batch, dilation, strided_len, GROUP_WIDTH), BF16),
            jax.ShapeDtypeStruct((batch, dilation, strided_len, LSE_LANES), F32),
        ],
        compiler_params=_compiler_params(3),
        name=f"dilated_attn_d{dilation}",
    )(qkv_g, qkv_g, qkv_g, qkv_g, qkv_g)


def _conv_kernel(x_ref, g_ref, wxbc_ref, wgc_ref, wco_ref, wconv_ref, out_ref,
                 ubuf_ref, yc_ref, *, tiles_per_seq, chunk):
    i = pl.program_id(0)
    tm = x_ref.shape[0]
    halo = CONV_HALO_ROWS

    @pl.when(i % tiles_per_seq == 0)
    def _():
        ubuf_ref[0:halo, :] = jnp.zeros((halo, CONV_WIDTH), F32)

    h = _rmsnorm_f32(x_ref[...], g_ref[...]).astype(BF16)
    for c in range(CONV_WIDTH // chunk):
        cols = slice(c * chunk, (c + 1) * chunk)
        xc = jnp.dot(h, wxbc_ref[:, cols], preferred_element_type=F32)
        cg = jnp.dot(h, wxbc_ref[:, 2 * CONV_WIDTH + c * chunk:2 * CONV_WIDTH + (c + 1) * chunk],
                     preferred_element_type=F32)
        u = cg * xc
        ubuf_ref[halo:halo + tm, cols] = u
        conv = (wconv_ref[2:3, cols] * u
                + wconv_ref[1:2, cols] * ubuf_ref[halo - 1:halo - 1 + tm, cols]
                + wconv_ref[0:1, cols] * ubuf_ref[halo - 2:halo - 2 + tm, cols])
        ubuf_ref[0:halo, cols] = ubuf_ref[tm:tm + halo, cols]
        bg = jnp.dot(h, wxbc_ref[:, CONV_WIDTH + c * chunk:CONV_WIDTH + (c + 1) * chunk],
                     preferred_element_type=F32)
        yb = (bg * conv).astype(BF16)
        part = jnp.dot(yb, wco_ref[cols, :], preferred_element_type=F32)
        if c == 0:
            yc_ref[...] = part
        else:
            yc_ref[...] += part
    for c in range(D_MODEL // chunk):
        cols = slice(c * chunk, (c + 1) * chunk)
        gate = jnp.dot(h, wgc_ref[:, cols], preferred_element_type=F32)
        out_ref[:, cols] = (jax.nn.sigmoid(gate) * yc_ref[:, cols]).astype(BF16)


def _conv_branch(x1, gain, w_xbc, w_gc, w_co, w_conv, *, seq, tm, chunk):
    t = x1.shape[0]
    return pl.pallas_call(
        functools.partial(_conv_kernel, tiles_per_seq=seq // tm, chunk=chunk),
        grid=(t // tm,),
        in_specs=[
            pl.BlockSpec((tm, D_MODEL), lambda i: (i, 0)),
            _resident((1, D_MODEL)),
            _resident((D_MODEL, XBC_COLS)),
            _resident((D_MODEL, D_MODEL)),
            _resident((CONV_WIDTH, D_MODEL)),
            _resident((CONV_KERNEL, CONV_WIDTH)),
        ],
        out_specs=pl.BlockSpec((tm, D_MODEL), lambda i: (i, 0)),
        out_shape=jax.ShapeDtypeStruct((t, D_MODEL), BF16),
        scratch_shapes=[
            pltpu.VMEM((tm + CONV_HALO_ROWS, CONV_WIDTH), F32),
            pltpu.VMEM((tm, D_MODEL), F32),
        ],
        compiler_params=_compiler_params(1),
        name="gated_short_conv",
    )(x1, gain, w_xbc, w_gc, w_co, w_conv)


def _merge_kernel(x_ref, g_ref, conv_ref, o0_ref, o1_ref, o2_ref, l0_ref, l1_ref, l2_ref,
                  wga_ref, wao_ref, wo_ref, out_ref, onat_ref, lnat_ref, oc_ref, merged_ref,
                  *, chunk):
    tm = x_ref.shape[0]
    x = x_ref[...]
    h = _rmsnorm_f32(x, g_ref[...]).astype(BF16)

    o_refs = (o0_ref, o1_ref, o2_ref)
    l_refs = (l0_ref, l1_ref, l2_ref)
    lses = []
    for g, (_, dilation) in enumerate(ATTN_PATTERNS):
        if dilation == 1:
            lses.append(l_refs[g][0])
            continue
        for r in range(dilation):
            dst = pl.ds(r, tm // dilation, stride=dilation)
            lnat_ref[g, dst, :] = l_refs[g][r]
            for hh in range(HEADS_PER_GROUP):
                cols = slice(hh * HEAD_DIM, (hh + 1) * HEAD_DIM)
                onat_ref[g * HEADS_PER_GROUP + hh, dst, :] = o_refs[g][r, :, cols].astype(F32)
        lses.append(lnat_ref[g])

    m = jnp.maximum(jnp.maximum(lses[0], lses[1]), lses[2])
    es = [jnp.exp(l - m) for l in lses]
    den = es[0] + es[1] + es[2]
    ws = [e / den for e in es]
    for hh in range(HEADS_PER_GROUP):
        cols = slice(hh * HEAD_DIM, (hh + 1) * HEAD_DIM)
        lane0 = hh * LSE_LANES_PER_HEAD
        acc = None
        for g, (_, dilation) in enumerate(ATTN_PATTERNS):
            if dilation == 1:
                o = o_refs[g][0, :, cols].astype(F32)
            else:
                o = onat_ref[g * HEADS_PER_GROUP + hh]
            term = ws[g][:, lane0:lane0 + 1] * o
            acc = term if acc is None else acc + term
        oc_ref[:, cols] = acc.astype(BF16)

    ya = jnp.dot(oc_ref[...], wao_ref[...], preferred_element_type=F32)
    for c in range(D_MODEL // chunk):
        cols = slice(c * chunk, (c + 1) * chunk)
        gate = jnp.dot(h, wga_ref[:, cols], preferred_element_type=F32)
        merged = conv_ref[:, cols].astype(F32) + jax.nn.sigmoid(gate) * ya[:, cols]
        merged_ref[:, cols] = merged.astype(BF16)
    out_ref[...] = x + jnp.dot(merged_ref[...], wo_ref[...], preferred_element_type=F32)


def _merge(x1, gain, conv, outs, lses, w_ga, w_ao, w_o, *, seq, tm, chunk):
    t = x1.shape[0]
    tiles_per_seq = seq // tm
    row = lambda width: pl.BlockSpec((tm, width), lambda i: (i, 0))

    def residue_rows(width):
        return [
            pl.BlockSpec((None, d, tm // d, width),
                         lambda i: (i // tiles_per_seq, 0, i % tiles_per_seq, 0))
            for _, d in ATTN_PATTERNS
        ]

    return pl.pallas_call(
        functools.partial(_merge_kernel, chunk=chunk),
        grid=(t // tm,),
        in_specs=[
            row(D_MODEL), _resident((1, D_MODEL)), row(D_MODEL),
            *residue_rows(GROUP_WIDTH), *residue_rows(LSE_LANES),
            _resident((D_MODEL, D_MODEL)),
            _resident((GROUP_WIDTH, D_MODEL)),
            _resident((D_MODEL, D_MODEL)),
        ],
        out_specs=row(D_MODEL),
        out_shape=jax.ShapeDtypeStruct((t, D_MODEL), F32),
        scratch_shapes=[
            pltpu.VMEM((N_GROUPS * HEADS_PER_GROUP, tm, HEAD_DIM), F32),
            pltpu.VMEM((N_GROUPS, tm, LSE_LANES), F32),
            pltpu.VMEM((tm, GROUP_WIDTH), BF16),
            pltpu.VMEM((tm, D_MODEL), BF16),
        ],
        compiler_params=_compiler_params(1),
        name="gated_merge",
    )(x1, gain, conv, *outs, *lses, w_ga, w_ao, w_o)


def _rotary_tables(seq):
    half = ROPE_DIM // 2
    inv_freq = ROPE_THETA ** (-(jnp.arange(half, dtype=F32) * 2.0) / ROPE_DIM)
    ang = jnp.arange(seq, dtype=jnp.int32).astype(F32)[:, None] * inv_freq[None, :]
    cos, sin = jnp.cos(ang), jnp.sin(ang)
    pad = HEAD_DIM - ROPE_DIM
    cos_tab = jnp.concatenate([cos, cos, jnp.ones((seq, pad), F32)], axis=-1)
    sin_tab = jnp.concatenate([-sin, sin, jnp.zeros((seq, pad), F32)], axis=-1)
    return cos_tab, sin_tab


@jax.jit
def kernel(x, ffn1_norm, w_ffn1_in, w_ffn1_out, mix_norm, w_in, w_conv, w_conv_out,
           w_attn_out, w_o, ffn2_norm, w_ffn2_in, w_ffn2_out, final_norm):
    batch, seq, d = x.shape
    depth = w_in.shape[0]
    xt = x.reshape(batch * seq, d)
    cos_tab, sin_tab = _rotary_tables(seq)
    final_gain = final_norm.reshape(1, d)
    q_end, xbc_end, gc_end = QKV_COLS, QKV_COLS + XBC_COLS, QKV_COLS + XBC_COLS + D_MODEL
    for l in range(depth):
        xt = _ffn(xt, ffn1_norm[l].reshape(1, d), w_ffn1_in[l].astype(BF16),
                  w_ffn1_out[l].astype(BF16), final_gain, final_norm=False, tm=1024, tf=512)
        mix_gain = mix_norm[l].reshape(1, d)
        w_in_l = w_in[l]
        qkv_groups = _qkv(xt, mix_gain, w_in_l[:, :q_end].astype(BF16), cos_tab, sin_tab,
                          batch=batch, seq=seq, tm=512)
        outs, lses = [], []
        for qkv_g, (window, dilation) in zip(qkv_groups, ATTN_PATTERNS):
            assert window // dilation == ATTN_BLOCK
            o, lse = _attn_group(qkv_g, dilation=dilation, tq=512)
            outs.append(o)
            lses.append(lse)
        conv = _conv_branch(xt, mix_gain, w_in_l[:, q_end:xbc_end].astype(BF16),
                            w_in_l[:, xbc_end:gc_end].astype(BF16), w_conv_out[l].astype(BF16),
                            w_conv[l], seq=seq, tm=512, chunk=256)
        xt = _merge(xt, mix_gain, conv, outs, lses, w_in_l[:, gc_end:].astype(BF16),
                    w_attn_out[l].astype(BF16), w_o[l].astype(BF16), seq=seq, tm=256, chunk=512)
        xt = _ffn(xt, ffn2_norm[l].reshape(1, d), w_ffn2_in[l].astype(BF16),
                  w_ffn2_out[l].astype(BF16), final_gain, final_norm=(l == depth - 1),
                  tm=1024, tf=512)
    return xt.reshape(batch, seq, d)
```

```python
import functools

import jax
import jax.numpy as jnp
from jax import lax
from jax.experimental import pallas as pl
from jax.experimental.pallas import tpu as pltpu

D_MODEL = 2048
HEAD_DIM = 128
HEADS_PER_GROUP = 4
ATTN_PATTERNS = ((128, 1), (512, 4), (2048, 16))
N_GROUPS = len(ATTN_PATTERNS)
ATTN_WIDTH = N_GROUPS * HEADS_PER_GROUP * HEAD_DIM
GROUP_WIDTH = HEADS_PER_GROUP * HEAD_DIM
ATTN_BLOCK = 128
ROPE_THETA = 500000.0
ROPE_DIM = HEAD_DIM // 4
CONV_WIDTH = D_MODEL // 2
CONV_KERNEL = 3
D_FF = 5632
NORM_EPS = 1e-5
QKV_COLS = 3 * ATTN_WIDTH
XBC_COLS = 3 * CONV_WIDTH

V7X_VMEM_LIMIT_BYTES = 58 * 1024 * 1024
LSE_LANES = 128
LSE_LANES_PER_HEAD = LSE_LANES // HEADS_PER_GROUP
CONV_HALO_ROWS = 8

BF16 = jnp.bfloat16
F32 = jnp.float32


def _rmsnorm_f32(x, g):
    return (x * lax.rsqrt(jnp.mean(x * x, axis=-1, keepdims=True) + NORM_EPS)) * g


def _store_rmsnorm(x_ref, g_ref, h_ref):
    h_ref[...] = _rmsnorm_f32(x_ref[...], g_ref[...]).astype(BF16)


def _compiler_params(n_axes):
    return pltpu.CompilerParams(
        dimension_semantics=("arbitrary",) * n_axes,
        vmem_limit_bytes=V7X_VMEM_LIMIT_BYTES,
    )


def _resident(shape):
    return pl.BlockSpec(shape, lambda *_: (0,) * len(shape), pipeline_mode=pl.Buffered(1))


def _resident_cols(width, col0):
    return pl.BlockSpec((pl.Element(D_MODEL), pl.Element(width)), lambda *_: (0, col0),
                        pipeline_mode=pl.Buffered(1))


def _ffn_kernel(x_ref, g_ref, wg_ref, wu_ref, wo_ref, gf_ref, out_ref, h_ref, *, final_norm):
    j = pl.program_id(1)

    @pl.when(j == 0)
    def _():
        x = x_ref[...]
        h_ref[...] = _rmsnorm_f32(x, g_ref[...]).astype(BF16)
        out_ref[...] = x

    gate = jnp.dot(h_ref[...], wg_ref[...], preferred_element_type=F32)
    up = jnp.dot(h_ref[...], wu_ref[...], preferred_element_type=F32)
    a = (0.5 * (gate * jax.nn.sigmoid(gate)) * up).astype(BF16)
    out_ref[...] += jnp.dot(a, wo_ref[...], preferred_element_type=F32)

    if final_norm:

        @pl.when(j == pl.num_programs(1) - 1)
        def _():
            out_ref[...] = _rmsnorm_f32(out_ref[...], gf_ref[...])


def _ffn(x, gain, w_in, w_out, final_gain, *, final_norm, tm, tf):
    t = x.shape[0]
    nf = D_FF // tf
    return pl.pallas_call(
        functools.partial(_ffn_kernel, final_norm=final_norm),
        grid=(t // tm, nf),
        in_specs=[
            pl.BlockSpec((tm, D_MODEL), lambda i, j: (i, 0)),
            pl.BlockSpec((1, D_MODEL), lambda i, j: (0, 0)),
            pl.BlockSpec((D_MODEL, tf), lambda i, j: (0, j)),
            pl.BlockSpec((D_MODEL, tf), lambda i, j: (0, j + nf)),
            pl.BlockSpec((tf, D_MODEL), lambda i, j: (j, 0)),
            pl.BlockSpec((1, D_MODEL), lambda i, j: (0, 0)),
        ],
        out_specs=pl.BlockSpec((tm, D_MODEL), lambda i, j: (i, 0)),
        out_shape=jax.ShapeDtypeStruct((t, D_MODEL), F32),
        scratch_shapes=[pltpu.VMEM((tm, D_MODEL), BF16)],
        compiler_params=_compiler_params(2),
        name="ffn_final" if final_norm else "ffn",
    )(x, gain, w_in, w_in, w_out, final_gain)


def _qkv_kernel(x_ref, g_ref, w_ref, cos_ref, sin_ref, *refs):
    out_refs, slab_ref, h_ref = refs[:N_GROUPS], refs[N_GROUPS], refs[N_GROUPS + 1]
    tm = x_ref.shape[0]
    _store_rmsnorm(x_ref, g_ref, h_ref)
    c = cos_ref[...]
    s = sin_ref[...]
    first_half = lax.broadcasted_iota(jnp.int32, c.shape, 1) < ROPE_DIM // 2
    slab = 0
    for g, (_, dilation) in sorted(enumerate(ATTN_PATTERNS), key=lambda p: -p[1][1]):
        for kind in range(3):
            col0 = kind * ATTN_WIDTH + g * GROUP_WIDTH
            acc = jnp.dot(h_ref[...], w_ref[:, col0:col0 + GROUP_WIDTH],
                          preferred_element_type=F32)
            for hh in range(HEADS_PER_GROUP):
                xh = acc[:, hh * HEAD_DIM:(hh + 1) * HEAD_DIM]
                if kind < 2:
                    nxt = pltpu.roll(xh, HEAD_DIM - ROPE_DIM // 2, axis=1)
                    prv = pltpu.roll(xh, ROPE_DIM // 2, axis=1)
                    xh = xh * c + jnp.where(first_half, nxt, prv) * s
                cols = slice(kind * GROUP_WIDTH + hh * HEAD_DIM, kind * GROUP_WIDTH + (hh + 1) * HEAD_DIM)
                if dilation == 1:
                    out_refs[g][0, :, cols] = xh.astype(BF16)
                else:
                    slab_ref[slab] = xh
                    for r in range(dilation):
                        rows = slab_ref[slab, pl.ds(r, tm // dilation, stride=dilation), :]
                        out_refs[g][r, :, cols] = rows.astype(BF16)
                    slab += 1


def _qkv(x1, gain, w_in, cos_tab, sin_tab, *, batch, seq, tm):
    t = x1.shape[0]
    tiles_per_seq = seq // tm
    n_slabs = 3 * HEADS_PER_GROUP * sum(1 for _, d in ATTN_PATTERNS if d > 1)
    return pl.pallas_call(
        _qkv_kernel,
        grid=(t // tm,),
        in_specs=[
            pl.BlockSpec((tm, D_MODEL), lambda i: (i, 0)),
            _resident((1, D_MODEL)),
            _resident_cols(QKV_COLS, 0),
            pl.BlockSpec((tm, HEAD_DIM), lambda i: (i % tiles_per_seq, 0)),
            pl.BlockSpec((tm, HEAD_DIM), lambda i: (i % tiles_per_seq, 0)),
        ],
        out_specs=[
            pl.BlockSpec((None, d, tm // d, 3 * GROUP_WIDTH),
                         lambda i: (i // tiles_per_seq, 0, i % tiles_per_seq, 0))
            for _, d in ATTN_PATTERNS
        ],
        out_shape=[
            jax.ShapeDtypeStruct((batch, d, seq // d, 3 * GROUP_WIDTH), BF16)
            for _, d in ATTN_PATTERNS
        ],
        scratch_shapes=[
            pltpu.VMEM((n_slabs, tm, HEAD_DIM), F32),
            pltpu.VMEM((tm, D_MODEL), BF16),
        ],
        compiler_params=_compiler_params(1),
        name="qkv_rotary",
    )(x1, gain, w_in, cos_tab, sin_tab)


def _attn_kernel(q_ref, kc_ref, kp_ref, vc_ref, vp_ref, o_ref, lse_ref):
    n = pl.program_id(2)
    tq = q_ref.shape[0]
    scale = HEAD_DIM ** -0.5
    qi = lax.broadcasted_iota(jnp.int32, (ATTN_BLOCK, 2 * ATTN_BLOCK), 0)
    kj = lax.broadcasted_iota(jnp.int32, (ATTN_BLOCK, 2 * ATTN_BLOCK), 1)
    band = jnp.logical_and(kj >= qi, kj <= qi + ATTN_BLOCK)
    first_band = jnp.logical_and(band, jnp.logical_or(kj >= ATTN_BLOCK, n > 0))
    lane = lax.broadcasted_iota(jnp.int32, (ATTN_BLOCK, LSE_LANES), 1)
    nt = (((1,), (1,)), ((), ()))

    for qb in range(tq // ATTN_BLOCK):
        rows = slice(qb * ATTN_BLOCK, (qb + 1) * ATTN_BLOCK)
        window = slice((qb - 1) * ATTN_BLOCK, (qb + 1) * ATTN_BLOCK)
        lse_tile = jnp.zeros((ATTN_BLOCK, LSE_LANES), F32)
        for hh in range(HEADS_PER_GROUP):
            cols = slice(hh * HEAD_DIM, (hh + 1) * HEAD_DIM)
            q = q_ref[rows, cols]
            if qb == 0:
                k2 = jnp.concatenate([kp_ref[:, cols], kc_ref[rows, cols]], axis=0)
                v2 = jnp.concatenate([vp_ref[:, cols], vc_ref[rows, cols]], axis=0)
                mask = first_band
            else:
                k2 = kc_ref[window, cols]
                v2 = vc_ref[window, cols]
                mask = band
            s = lax.dot_general(q, k2, nt, preferred_element_type=F32) * scale
            s = jnp.where(mask, s, -jnp.inf)
            m = jnp.max(s, axis=-1, keepdims=True)
            p = jnp.exp(s - m)
            l = jnp.sum(p, axis=-1, keepdims=True)
            pv = jnp.dot(p.astype(BF16), v2, preferred_element_type=F32)
            o_ref[rows, cols] = (pv / l).astype(BF16)
            lse_tile = jnp.where(lane // LSE_LANES_PER_HEAD == hh, m + jnp.log(l), lse_tile)
        lse_ref[rows, :] = lse_tile


def _attn_group(qkv_g, *, dilation, tq):
    batch, _, strided_len, _ = qkv_g.shape
    tq = min(tq, strided_len)
    sub = tq // ATTN_BLOCK

    def cur(kind):
        return pl.BlockSpec((None, None, tq, GROUP_WIDTH), lambda b, r, n: (b, r, n, kind))

    def prev(kind):
        return pl.BlockSpec((None, None, ATTN_BLOCK, GROUP_WIDTH),
                            lambda b, r, n: (b, r, jnp.maximum(n * sub - 1, 0), kind))

    return pl.pallas_call(
        _attn_kernel,
        grid=(batch, dilation, strided_len // tq),
        in_specs=[cur(0), cur(1), prev(1), cur(2), prev(2)],
        out_specs=[
            pl.BlockSpec((None, None, tq, GROUP_WIDTH), lambda b, r, n: (b, r, n, 0)),
            pl.BlockSpec((None, None, tq, LSE_LANES), lambda b, r, n: (b, r, n, 0)),
        ],
        out_shape=[
            jax.ShapeDtypeStruct((batch, dilation, strided_len, GROUP_WIDTH), BF16),
            jax.ShapeDtypeStruct((batch, dilation, strided_len, LSE_LANES), F32),
        ],
        compiler_params=_compiler_params(3),
        name=f"dilated_attn_d{dilation}",
    )(qkv_g, qkv_g, qkv_g, qkv_g, qkv_g)


def _conv_kernel(x_ref, g_ref, wxbc_ref, wgc_ref, wco_ref, wconv_ref, out_ref,
                 ubuf_ref, yc_ref, h_ref, *, tiles_per_seq, chunk):
    i = pl.program_id(0)
    tm = x_ref.shape[0]
    halo = CONV_HALO_ROWS

    @pl.when(i % tiles_per_seq == 0)
    def _():
        ubuf_ref[0:halo, :] = jnp.zeros((halo, CONV_WIDTH), F32)

    _store_rmsnorm(x_ref, g_ref, h_ref)
    for c in range(CONV_WIDTH // chunk):
        cols = slice(c * chunk, (c + 1) * chunk)
        xc = jnp.dot(h_ref[...], wxbc_ref[:, cols], preferred_element_type=F32)
        cg = jnp.dot(h_ref[...], wxbc_ref[:, 2 * CONV_WIDTH + c * chunk:2 * CONV_WIDTH + (c + 1) * chunk],
                     preferred_element_type=F32)
        u = cg * xc
        ubuf_ref[halo:halo + tm, cols] = u
        conv = (wconv_ref[2:3, cols] * u
                + wconv_ref[1:2, cols] * ubuf_ref[halo - 1:halo - 1 + tm, cols]
                + wconv_ref[0:1, cols] * ubuf_ref[halo - 2:halo - 2 + tm, cols])
        ubuf_ref[0:halo, cols] = ubuf_ref[tm:tm + halo, cols]
        bg = jnp.dot(h_ref[...], wxbc_ref[:, CONV_WIDTH + c * chunk:CONV_WIDTH + (c + 1) * chunk],
                     preferred_element_type=F32)
        yb = (bg * conv).astype(BF16)
        part = jnp.dot(yb, wco_ref[cols, :], preferred_element_type=F32)
        if c == 0:
            yc_ref[...] = part
        else:
            yc_ref[...] += part
    for c in range(D_MODEL // chunk):
        cols = slice(c * chunk, (c + 1) * chunk)
        gate = jnp.dot(h_ref[...], wgc_ref[:, cols], preferred_element_type=F32)
        out_ref[:, cols] = (jax.nn.sigmoid(gate) * yc_ref[:, cols]).astype(BF16)


def _conv_branch(x1, gain, w_in, w_co, w_conv, *, xbc_col0, gc_col0, seq, tm, chunk):
    t = x1.shape[0]
    return pl.pallas_call(
        functools.partial(_conv_kernel, tiles_per_seq=seq // tm, chunk=chunk),
        grid=(t // tm,),
        in_specs=[
            pl.BlockSpec((tm, D_MODEL), lambda i: (i, 0)),
            _resident((1, D_MODEL)),
            _resident_cols(XBC_COLS, xbc_col0),
            _resident_cols(D_MODEL, gc_col0),
            _resident((CONV_WIDTH, D_MODEL)),
            _resident((CONV_KERNEL, CONV_WIDTH)),
        ],
        out_specs=pl.BlockSpec((tm, D_MODEL), lambda i: (i, 0)),
        out_shape=jax.ShapeDtypeStruct((t, D_MODEL), BF16),
        scratch_shapes=[
            pltpu.VMEM((tm + CONV_HALO_ROWS, CONV_WIDTH), F32),
            pltpu.VMEM((tm, D_MODEL), F32),
            pltpu.VMEM((tm, D_MODEL), BF16),
        ],
        compiler_params=_compiler_params(1),
        name="gated_short_conv",
    )(x1, gain, w_in, w_in, w_co, w_conv)


def _merge_kernel(x_ref, g_ref, conv_ref, o0_ref, o1_ref, o2_ref, l0_ref, l1_ref, l2_ref,
                  wga_ref, wao_ref, wo_ref, out_ref, onat_ref, lnat_ref, oc_ref, h_ref,
                  *, chunk):
    tm = x_ref.shape[0]
    _store_rmsnorm(x_ref, g_ref, h_ref)

    o_refs = (o0_ref, o1_ref, o2_ref)
    l_refs = (l0_ref, l1_ref, l2_ref)
    lses = []
    for g, (_, dilation) in enumerate(ATTN_PATTERNS):
        if dilation == 1:
            lses.append(l_refs[g][0])
            continue
        for r in range(dilation):
            dst = pl.ds(r, tm // dilation, stride=dilation)
            lnat_ref[g - 1, dst, :] = l_refs[g][r]
            for hh in range(HEADS_PER_GROUP):
                cols = slice(hh * HEAD_DIM, (hh + 1) * HEAD_DIM)
                onat_ref[(g - 1) * HEADS_PER_GROUP + hh, dst, :] = o_refs[g][r, :, cols].astype(F32)
        lses.append(lnat_ref[g - 1])

    m = jnp.maximum(jnp.maximum(lses[0], lses[1]), lses[2])
    es = [jnp.exp(l - m) for l in lses]
    den = es[0] + es[1] + es[2]
    ws = [e / den for e in es]
    for hh in range(HEADS_PER_GROUP):
        cols = slice(hh * HEAD_DIM, (hh + 1) * HEAD_DIM)
        lane0 = hh * LSE_LANES_PER_HEAD
        acc = None
        for g, (_, dilation) in enumerate(ATTN_PATTERNS):
            if dilation == 1:
                o = o_refs[g][0, :, cols].astype(F32)
            else:
                o = onat_ref[(g - 1) * HEADS_PER_GROUP + hh]
            term = ws[g][:, lane0:lane0 + 1] * o
            acc = term if acc is None else acc + term
        oc_ref[:, cols] = acc.astype(BF16)

    for c in range(D_MODEL // chunk):
        cols = slice(c * chunk, (c + 1) * chunk)
        gate = jnp.dot(h_ref[...], wga_ref[:, cols], preferred_element_type=F32)
        ya = jnp.dot(oc_ref[...], wao_ref[:, cols], preferred_element_type=F32)
        merged = (conv_ref[:, cols].astype(F32) + jax.nn.sigmoid(gate) * ya).astype(BF16)
        part = jnp.dot(merged, wo_ref[cols, :], preferred_element_type=F32)
        if c == 0:
            out_ref[...] = x_ref[...] + part
        else:
            out_ref[...] += part


def _merge(x1, gain, conv, outs, lses, w_in, w_ao, w_o, *, ga_col0, seq, tm, chunk):
    t = x1.shape[0]
    tiles_per_seq = seq // tm
    n_dilated = N_GROUPS - 1
    assert ATTN_PATTERNS[0][1] == 1 and all(d > 1 for _, d in ATTN_PATTERNS[1:])
    row =lambda width: pl.BlockSpec((tm, width), lambda i: (i, 0))

    def residue_rows(width):
        return [
            pl.BlockSpec((None, d, tm // d, width),
                         lambda i: (i // tiles_per_seq, 0, i % tiles_per_seq, 0))
            for _, d in ATTN_PATTERNS
        ]

    return pl.pallas_call(
        functools.partial(_merge_kernel, chunk=chunk),
        grid=(t // tm,),
        in_specs=[
            row(D_MODEL), _resident((1, D_MODEL)), row(D_MODEL),
            *residue_rows(GROUP_WIDTH), *residue_rows(LSE_LANES),
            _resident_cols(D_MODEL, ga_col0),
            _resident((GROUP_WIDTH, D_MODEL)),
            _resident((D_MODEL, D_MODEL)),
        ],
        out_specs=row(D_MODEL),
        out_shape=jax.ShapeDtypeStruct((t, D_MODEL), F32),
        scratch_shapes=[
            pltpu.VMEM((n_dilated * HEADS_PER_GROUP, tm, HEAD_DIM), F32),
            pltpu.VMEM((n_dilated, tm, LSE_LANES), F32),
            pltpu.VMEM((tm, GROUP_WIDTH), BF16),
            pltpu.VMEM((tm, D_MODEL), BF16),
        ],
        compiler_params=_compiler_params(1),
        name="gated_merge",
    )(x1, gain, conv, *outs, *lses, w_in, w_ao, w_o)


def _rotary_tables(seq):
    half = ROPE_DIM // 2
    inv_freq = ROPE_THETA ** (-(jnp.arange(half, dtype=F32) * 2.0) / ROPE_DIM)
    rest = HEAD_DIM - ROPE_DIM
    freq = jnp.concatenate([inv_freq, inv_freq, jnp.zeros((rest,), F32)])
    sign = jnp.concatenate([-jnp.ones((half,), F32), jnp.ones((half,), F32),
                            jnp.zeros((rest,), F32)])
    ang = jnp.arange(seq, dtype=jnp.int32).astype(F32)[:, None] * freq[None, :]
    return jnp.cos(ang), jnp.sin(ang) * sign[None, :]


@jax.jit
def kernel(x, ffn1_norm, w_ffn1_in, w_ffn1_out, mix_norm, w_in, w_conv, w_conv_out,
           w_attn_out, w_o, ffn2_norm, w_ffn2_in, w_ffn2_out, final_norm):
    batch, seq, d = x.shape
    depth = w_in.shape[0]
    xt = x.reshape(batch * seq, d)
    cos_tab, sin_tab = _rotary_tables(seq)
    final_gain = final_norm.reshape(1, d)
    q_end, xbc_end, gc_end = QKV_COLS, QKV_COLS + XBC_COLS, QKV_COLS + XBC_COLS + D_MODEL
    for l in range(depth):
        xt = _ffn(xt, ffn1_norm[l].reshape(1, d), w_ffn1_in[l].astype(BF16),
                  w_ffn1_out[l].astype(BF16), final_gain, final_norm=False, tm=1024, tf=512)
        mix_gain = mix_norm[l].reshape(1, d)
        w_in_l = w_in[l].astype(BF16)
        qkv_groups = _qkv(xt, mix_gain, w_in_l, cos_tab, sin_tab, batch=batch, seq=seq, tm=512)
        outs, lses = [], []
        for qkv_g, (window, dilation) in zip(qkv_groups, ATTN_PATTERNS):
            assert window // dilation == ATTN_BLOCK
            o, lse = _attn_group(qkv_g, dilation=dilation, tq=1024)
            outs.append(o)
            lses.append(lse)
        conv = _conv_branch(xt, mix_gain, w_in_l, w_conv_out[l].astype(BF16), w_conv[l],
                            xbc_col0=q_end, gc_col0=xbc_end, seq=seq, tm=512, chunk=512)
        xt = _merge(xt, mix_gain, conv, outs, lses, w_in_l, w_attn_out[l].astype(BF16),
                    w_o[l].astype(BF16), ga_col0=gc_end, seq=seq, tm=512, chunk=512)
        xt = _ffn(xt, ffn2_norm[l].reshape(1, d), w_ffn2_in[l].astype(BF16),
                  w_ffn2_out[l].astype(BF16), final_gain, final_norm=(l == depth - 1),
                  tm=1024, tf=512)
    return xt.reshape(batch, seq, d)
```

```python
import functools

import jax
import jax.numpy as jnp
from jax import lax
from jax.experimental import pallas as pl
from jax.experimental.pallas import tpu as pltpu

D_MODEL = 2048
HEAD_DIM = 128
HEADS_PER_GROUP = 4
ATTN_PATTERNS = ((128, 1), (512, 4), (2048, 16))
N_GROUPS = len(ATTN_PATTERNS)
ATTN_WIDTH = N_GROUPS * HEADS_PER_GROUP * HEAD_DIM
GROUP_WIDTH = HEADS_PER_GROUP * HEAD_DIM
ATTN_BLOCK = 128
ROPE_THETA = 500000.0
ROPE_DIM = HEAD_DIM // 4
CONV_WIDTH = D_MODEL // 2
CONV_KERNEL = 3
D_FF = 5632
NORM_EPS = 1e-5
QKV_COLS = 3 * ATTN_WIDTH
XBC_COLS = 3 * CONV_WIDTH

V7X_VMEM_LIMIT_BYTES = 58 * 1024 * 1024
LSE_LANES = 128
LSE_LANES_PER_HEAD = LSE_LANES // HEADS_PER_GROUP
CONV_HALO_ROWS = 8

BF16 = jnp.bfloat16
F32 = jnp.float32


def _rmsnorm_f32(x, g):
    return (x * lax.rsqrt(jnp.mean(x * x, axis=-1, keepdims=True) + NORM_EPS)) * g


def _store_rmsnorm(x_ref, g_ref, h_ref):
    h_ref[...] = _rmsnorm_f32(x_ref[...], g_ref[...]).astype(BF16)


def _compiler_params(n_axes):
    return pltpu.CompilerParams(
        dimension_semantics=("arbitrary",) * n_axes,
        vmem_limit_bytes=V7X_VMEM_LIMIT_BYTES,
    )


def _resident(shape):
    return pl.BlockSpec(shape, lambda *_: (0,) * len(shape), pipeline_mode=pl.Buffered(1))


def _resident_cols(width, col0):
    return pl.BlockSpec((pl.Element(D_MODEL), pl.Element(width)), lambda *_: (0, col0),
                        pipeline_mode=pl.Buffered(1))


def _cast_stream_specs(shape, block, n_steps, step_of):
    n_col_blocks = shape[1] // block[1]
    n_blocks = (shape[0] // block[0]) * n_col_blocks
    assert shape[0] % block[0] == 0 and shape[1] % block[1] == 0 and n_blocks <= n_steps

    def index(*ids):
        b = jnp.minimum(step_of(*ids), n_blocks - 1)
        return (b // n_col_blocks, b % n_col_blocks)

    return pl.BlockSpec(block, index), pl.BlockSpec(block, index)


def _cast_blocks(src_refs, dst_refs):
    for src, dst in zip(src_refs, dst_refs):
        dst[...] = src[...].astype(BF16)


def _ffn_kernel(x_ref, g_ref, wg_ref, wu_ref, wo_ref, gf_ref, *refs, final_norm, n_casts):
    cast_src, (out_ref, *cast_dst), h_ref = refs[:n_casts], refs[n_casts:-1], refs[-1]
    j = pl.program_id(1)

    @pl.when(j == 0)
    def _():
        x = x_ref[...]
        h_ref[...] = _rmsnorm_f32(x, g_ref[...]).astype(BF16)
        out_ref[...] = x

    gate = jnp.dot(h_ref[...], wg_ref[...], preferred_element_type=F32)
    up = jnp.dot(h_ref[...], wu_ref[...], preferred_element_type=F32)
    a = (0.5 * (gate * jax.nn.sigmoid(gate)) * up).astype(BF16)
    out_ref[...] += jnp.dot(a, wo_ref[...], preferred_element_type=F32)
    _cast_blocks(cast_src, cast_dst)

    if final_norm:

        @pl.when(j == pl.num_programs(1) - 1)
        def _():
            out_ref[...] = _rmsnorm_f32(out_ref[...], gf_ref[...])


def _ffn(x, gain, w_in, w_out, final_gain, casts=(), *, final_norm, tm, tf):
    t = x.shape[0]
    nf = D_FF // tf
    n_steps = (t // tm) * nf
    cast_specs = [_cast_stream_specs(w.shape, blk, n_steps, lambda i, j: i * nf + j)
                  for w, blk in casts]
    return pl.pallas_call(
        functools.partial(_ffn_kernel, final_norm=final_norm, n_casts=len(casts)),
        grid=(t // tm, nf),
        in_specs=[
            pl.BlockSpec((tm, D_MODEL), lambda i, j: (i, 0)),
            pl.BlockSpec((1, D_MODEL), lambda i, j: (0, 0)),
            pl.BlockSpec((D_MODEL, tf), lambda i, j: (0, j)),
            pl.BlockSpec((D_MODEL, tf), lambda i, j: (0, j + nf)),
            pl.BlockSpec((tf, D_MODEL), lambda i, j: (j, 0)),
            pl.BlockSpec((1, D_MODEL), lambda i, j: (0, 0)),
            *[src for src, _ in cast_specs],
        ],
        out_specs=[pl.BlockSpec((tm, D_MODEL), lambda i, j: (i, 0)),
                   *[dst for _, dst in cast_specs]],
        out_shape=[jax.ShapeDtypeStruct((t, D_MODEL), F32),
                   *[jax.ShapeDtypeStruct(w.shape, BF16) for w, _ in casts]],
        scratch_shapes=[pltpu.VMEM((tm, D_MODEL), BF16)],
        compiler_params=_compiler_params(2),
        name="ffn_final" if final_norm else "ffn",
    )(x, gain, w_in, w_in, w_out, final_gain, *[w for w, _ in casts])


def _qkv_kernel(x_ref, g_ref, w_ref, cos_ref, sin_ref, *refs):
    out_refs, slab_ref, h_ref = refs[:N_GROUPS], refs[N_GROUPS], refs[N_GROUPS + 1]
    tm = x_ref.shape[0]
    _store_rmsnorm(x_ref, g_ref, h_ref)
    c = cos_ref[...]
    s = sin_ref[...]
    first_half = lax.broadcasted_iota(jnp.int32, c.shape, 1) < ROPE_DIM // 2
    slab = 0
    for g, (_, dilation) in sorted(enumerate(ATTN_PATTERNS), key=lambda p: -p[1][1]):
        for kind in range(3):
            col0 = kind * ATTN_WIDTH + g * GROUP_WIDTH
            acc = jnp.dot(h_ref[...], w_ref[:, col0:col0 + GROUP_WIDTH],
                          preferred_element_type=F32)
            for hh in range(HEADS_PER_GROUP):
                xh = acc[:, hh * HEAD_DIM:(hh + 1) * HEAD_DIM]
                if kind < 2:
                    nxt = pltpu.roll(xh, HEAD_DIM - ROPE_DIM // 2, axis=1)
                    prv = pltpu.roll(xh, ROPE_DIM // 2, axis=1)
                    xh = xh * c + jnp.where(first_half, nxt, prv) * s
                cols = slice(kind * GROUP_WIDTH + hh * HEAD_DIM, kind * GROUP_WIDTH + (hh + 1) * HEAD_DIM)
                if dilation == 1:
                    out_refs[g][0, :, cols] = xh.astype(BF16)
                else:
                    slab_ref[slab] = xh
                    for r in range(dilation):
                        rows = slab_ref[slab, pl.ds(r, tm // dilation, stride=dilation), :]
                        out_refs[g][r, :, cols] = rows.astype(BF16)
                    slab += 1


def _qkv(x1, gain, w_in, cos_tab, sin_tab, *, batch, seq, tm):
    t = x1.shape[0]
    tiles_per_seq = seq // tm
    n_slabs = 3 * HEADS_PER_GROUP * sum(1 for _, d in ATTN_PATTERNS if d > 1)
    return pl.pallas_call(
        _qkv_kernel,
        grid=(t // tm,),
        in_specs=[
            pl.BlockSpec((tm, D_MODEL), lambda i: (i, 0)),
            _resident((1, D_MODEL)),
            _resident_cols(QKV_COLS, 0),
            pl.BlockSpec((tm, HEAD_DIM), lambda i: (i % tiles_per_seq, 0)),
            pl.BlockSpec((tm, HEAD_DIM), lambda i: (i % tiles_per_seq, 0)),
        ],
        out_specs=[
            pl.BlockSpec((None, d, tm // d, 3 * GROUP_WIDTH),
                         lambda i: (i // tiles_per_seq, 0, i % tiles_per_seq, 0))
            for _, d in ATTN_PATTERNS
        ],
        out_shape=[
            jax.ShapeDtypeStruct((batch, d, seq // d, 3 * GROUP_WIDTH), BF16)
            for _, d in ATTN_PATTERNS
        ],
        scratch_shapes=[
            pltpu.VMEM((n_slabs, tm, HEAD_DIM), F32),
            pltpu.VMEM((tm, D_MODEL), BF16),
        ],
        compiler_params=_compiler_params(1),
        name="qkv_rotary",
    )(x1, gain, w_in, cos_tab, sin_tab)


def _attn_kernel(q_ref, kc_ref, kp_ref, vc_ref, vp_ref, *refs, n_casts):
    cast_src, (o_ref, lse_ref, *cast_dst) = refs[:n_casts], refs[n_casts:]
    _cast_blocks(cast_src, cast_dst)
    n = pl.program_id(2)
    tq = q_ref.shape[0]
    scale = HEAD_DIM ** -0.5
    qi = lax.broadcasted_iota(jnp.int32, (ATTN_BLOCK, 2 * ATTN_BLOCK), 0)
    kj = lax.broadcasted_iota(jnp.int32, (ATTN_BLOCK, 2 * ATTN_BLOCK), 1)
    band = jnp.logical_and(kj >= qi, kj <= qi + ATTN_BLOCK)
    first_band = jnp.logical_and(band, jnp.logical_or(kj >= ATTN_BLOCK, n > 0))
    lane = lax.broadcasted_iota(jnp.int32, (ATTN_BLOCK, LSE_LANES), 1)
    nt = (((1,), (1,)), ((), ()))

    for qb in range(tq // ATTN_BLOCK):
        rows = slice(qb * ATTN_BLOCK, (qb + 1) * ATTN_BLOCK)
        window = slice((qb - 1) * ATTN_BLOCK, (qb + 1) * ATTN_BLOCK)
        lse_tile = jnp.zeros((ATTN_BLOCK, LSE_LANES), F32)
        for hh in range(HEADS_PER_GROUP):
            cols = slice(hh * HEAD_DIM, (hh + 1) * HEAD_DIM)
            q = q_ref[rows, cols]
            if qb == 0:
                k2 = jnp.concatenate([kp_ref[:, cols], kc_ref[rows, cols]], axis=0)
                v2 = jnp.concatenate([vp_ref[:, cols], vc_ref[rows, cols]], axis=0)
                mask = first_band
            else:
                k2 = kc_ref[window, cols]
                v2 = vc_ref[window, cols]
                mask = band
            s = lax.dot_general(q, k2, nt, preferred_element_type=F32) * scale
            s = jnp.where(mask, s, -jnp.inf)
            m = jnp.max(s, axis=-1, keepdims=True)
            p = jnp.exp(s - m)
            l = jnp.sum(p, axis=-1, keepdims=True)
            pv = jnp.dot(p.astype(BF16), v2, preferred_element_type=F32)
            o_ref[rows, cols] = (pv / l).astype(BF16)
            lse_tile = jnp.where(lane // LSE_LANES_PER_HEAD == hh, m + jnp.log(l), lse_tile)
        lse_ref[rows, :] = lse_tile


def _attn_group(qkv_g, casts=(), *, dilation, tq):
    batch, _, strided_len, _ = qkv_g.shape
    tq = min(tq, strided_len)
    sub = tq // ATTN_BLOCK
    n_q = strided_len // tq
    cast_specs = [
        _cast_stream_specs(w.shape, blk, batch * dilation * n_q,
                           lambda b, r, n: (b * dilation + r) * n_q + n)
        for w, blk in casts
    ]

    def cur(kind):
        return pl.BlockSpec((None, None, tq, GROUP_WIDTH), lambda b, r, n: (b, r, n, kind))

    def prev(kind):
        return pl.BlockSpec((None, None, ATTN_BLOCK, GROUP_WIDTH),
                            lambda b, r, n: (b, r, jnp.maximum(n * sub - 1, 0), kind))

    return pl.pallas_call(
        functools.partial(_attn_kernel, n_casts=len(casts)),
        grid=(batch, dilation, n_q),
        in_specs=[cur(0), cur(1), prev(1), cur(2), prev(2), *[src for src, _ in cast_specs]],
        out_specs=[
            pl.BlockSpec((None, None, tq, GROUP_WIDTH), lambda b, r, n: (b, r, n, 0)),
            pl.BlockSpec((None, None, tq, LSE_LANES), lambda b, r, n: (b, r, n, 0)),
            *[dst for _, dst in cast_specs],
        ],
        out_shape=[
            jax.ShapeDtypeStruct((batch, dilation, strided_len, GROUP_WIDTH), BF16),
            jax.ShapeDtypeStruct((batch, dilation, strided_len, LSE_LANES), F32),
            *[jax.ShapeDtypeStruct(w.shape, BF16) for w, _ in casts],
        ],
        compiler_params=_compiler_params(3),
        name=f"dilated_attn_d{dilation}",
    )(qkv_g, qkv_g, qkv_g, qkv_g, qkv_g, *[w for w, _ in casts])


def _conv_kernel(x_ref, g_ref, wxbc_ref, wgc_ref, wco_ref, wconv_ref, out_ref,
                 ubuf_ref, yc_ref, h_ref, *, tiles_per_seq, chunk):
    i = pl.program_id(0)
    tm = x_ref.shape[0]
    halo = CONV_HALO_ROWS

    @pl.when(i % tiles_per_seq == 0)
    def _():
        ubuf_ref[0:halo, :] = jnp.zeros((halo, CONV_WIDTH), F32)

    _store_rmsnorm(x_ref, g_ref, h_ref)
    for c in range(CONV_WIDTH // chunk):
        cols = slice(c * chunk, (c + 1) * chunk)
        xc = jnp.dot(h_ref[...], wxbc_ref[:, cols], preferred_element_type=F32)
        cg = jnp.dot(h_ref[...], wxbc_ref[:, 2 * CONV_WIDTH + c * chunk:2 * CONV_WIDTH + (c + 1) * chunk],
                     preferred_element_type=F32)
        u = cg * xc
        ubuf_ref[halo:halo + tm, cols] = u
        conv = (wconv_ref[2:3, cols] * u
                + wconv_ref[1:2, cols] * ubuf_ref[halo - 1:halo - 1 + tm, cols]
                + wconv_ref[0:1, cols] * ubuf_ref[halo - 2:halo - 2 + tm, cols])
        ubuf_ref[0:halo, cols] = ubuf_ref[tm:tm + halo, cols]
        bg = jnp.dot(h_ref[...], wxbc_ref[:, CONV_WIDTH + c * chunk:CONV_WIDTH + (c + 1) * chunk],
                     preferred_element_type=F32)
        yb = (bg * conv).astype(BF16)
        part = jnp.dot(yb, wco_ref[cols, :], preferred_element_type=F32)
        if c == 0:
            yc_ref[...] = part
        else:
            yc_ref[...] += part
    for c in range(D_MODEL // chunk):
        cols = slice(c * chunk, (c + 1) * chunk)
        gate = jnp.dot(h_ref[...], wgc_ref[:, cols], preferred_element_type=F32)
        out_ref[:, cols] = (jax.nn.sigmoid(gate) * yc_ref[:, cols]).astype(BF16)


def _conv_branch(x1, gain, w_in, w_co, w_conv, *, xbc_col0, gc_col0, seq, tm, chunk):
    t = x1.shape[0]
    return pl.pallas_call(
        functools.partial(_conv_kernel, tiles_per_seq=seq // tm, chunk=chunk),
        grid=(t // tm,),
        in_specs=[
            pl.BlockSpec((tm, D_MODEL), lambda i: (i, 0)),
            _resident((1, D_MODEL)),
            _resident_cols(XBC_COLS, xbc_col0),
            _resident_cols(D_MODEL, gc_col0),
            _resident((CONV_WIDTH, D_MODEL)),
            _resident((CONV_KERNEL, CONV_WIDTH)),
        ],
        out_specs=pl.BlockSpec((tm, D_MODEL), lambda i: (i, 0)),
        out_shape=jax.ShapeDtypeStruct((t, D_MODEL), BF16),
        scratch_shapes=[
            pltpu.VMEM((tm + CONV_HALO_ROWS, CONV_WIDTH), F32),
            pltpu.VMEM((tm, D_MODEL), F32),
            pltpu.VMEM((tm, D_MODEL), BF16),
        ],
        compiler_params=_compiler_params(1),
        name="gated_short_conv",
    )(x1, gain, w_in, w_in, w_co, w_conv)


def _merge_kernel(x_ref, g_ref, conv_ref, o0_ref, o1_ref, o2_ref, l0_ref, l1_ref, l2_ref,
                  wga_ref, wao_ref, wo_ref, out_ref, onat_ref, lnat_ref, oc_ref, h_ref,
                  *, chunk):
    tm = x_ref.shape[0]
    _store_rmsnorm(x_ref, g_ref, h_ref)

    o_refs = (o0_ref, o1_ref, o2_ref)
    l_refs = (l0_ref, l1_ref, l2_ref)
    lses = []
    for g, (_, dilation) in enumerate(ATTN_PATTERNS):
        if dilation == 1:
            lses.append(l_refs[g][0])
            continue
        for r in range(dilation):
            dst = pl.ds(r, tm // dilation, stride=dilation)
            lnat_ref[g - 1, dst, :] = l_refs[g][r]
            for hh in range(HEADS_PER_GROUP):
                cols = slice(hh * HEAD_DIM, (hh + 1) * HEAD_DIM)
                onat_ref[(g - 1) * HEADS_PER_GROUP + hh, dst, :] = o_refs[g][r, :, cols].astype(F32)
        lses.append(lnat_ref[g - 1])

    m = jnp.maximum(jnp.maximum(lses[0], lses[1]), lses[2])
    es = [jnp.exp(l - m) for l in lses]
    den = es[0] + es[1] + es[2]
    ws = [e / den for e in es]
    for hh in range(HEADS_PER_GROUP):
        cols = slice(hh * HEAD_DIM, (hh + 1) * HEAD_DIM)
        lane0 = hh * LSE_LANES_PER_HEAD
        acc = None
        for g, (_, dilation) in enumerate(ATTN_PATTERNS):
            if dilation == 1:
                o = o_refs[g][0, :, cols].astype(F32)
            else:
                o = onat_ref[(g - 1) * HEADS_PER_GROUP + hh]
            term = ws[g][:, lane0:lane0 + 1] * o
            acc = term if acc is None else acc + term
        oc_ref[:, cols] = acc.astype(BF16)

    for c in range(D_MODEL // chunk):
        cols = slice(c * chunk, (c + 1) * chunk)
        gate = jnp.dot(h_ref[...], wga_ref[:, cols], preferred_element_type=F32)
        ya = jnp.dot(oc_ref[...], wao_ref[:, cols], preferred_element_type=F32)
        merged = (conv_ref[:, cols].astype(F32) + jax.nn.sigmoid(gate) * ya).astype(BF16)
        part = jnp.dot(merged, wo_ref[cols, :], preferred_element_type=F32)
        if c == 0:
            out_ref[...] = x_ref[...] + part
        else:
            out_ref[...] += part


def _merge(x1, gain, conv, outs, lses, w_in, w_ao, w_o, *, ga_col0, seq, tm, chunk):
    t = x1.shape[0]
    tiles_per_seq = seq // tm
    n_dilated = N_GROUPS - 1
    assert ATTN_PATTERNS[0][1] == 1 and all(d > 1 for _, d in ATTN_PATTERNS[1:])
    row =lambda width: pl.BlockSpec((tm, width), lambda i: (i, 0))

    def residue_rows(width):
        return [
            pl.BlockSpec((None, d, tm // d, width),
                         lambda i: (i // tiles_per_seq, 0, i % tiles_per_seq, 0))
            for _, d in ATTN_PATTERNS
        ]

    return pl.pallas_call(
        functools.partial(_merge_kernel, chunk=chunk),
        grid=(t // tm,),
        in_specs=[
            row(D_MODEL), _resident((1, D_MODEL)), row(D_MODEL),
            *residue_rows(GROUP_WIDTH), *residue_rows(LSE_LANES),
            _resident_cols(D_MODEL, ga_col0),
            _resident((GROUP_WIDTH, D_MODEL)),
            _resident((D_MODEL, D_MODEL)),
        ],
        out_specs=row(D_MODEL),
        out_shape=jax.ShapeDtypeStruct((t, D_MODEL), F32),
        scratch_shapes=[
            pltpu.VMEM((n_dilated * HEADS_PER_GROUP, tm, HEAD_DIM), F32),
            pltpu.VMEM((n_dilated, tm, LSE_LANES), F32),
            pltpu.VMEM((tm, GROUP_WIDTH), BF16),
            pltpu.VMEM((tm, D_MODEL), BF16),
        ],
        compiler_params=_compiler_params(1),
        name="gated_merge",
    )(x1, gain, conv, *outs, *lses, w_in, w_ao, w_o)


def _rotary_tables(seq):
    half = ROPE_DIM // 2
    inv_freq = ROPE_THETA ** (-(jnp.arange(half, dtype=F32) * 2.0) / ROPE_DIM)
    rest = HEAD_DIM - ROPE_DIM
    freq = jnp.concatenate([inv_freq, inv_freq, jnp.zeros((rest,), F32)])
    sign = jnp.concatenate([-jnp.ones((half,), F32), jnp.ones((half,), F32),
                            jnp.zeros((rest,), F32)])
    ang = jnp.arange(seq, dtype=jnp.int32).astype(F32)[:, None] * freq[None, :]
    return jnp.cos(ang), jnp.sin(ang) * sign[None, :]


@jax.jit
def kernel(x, ffn1_norm, w_ffn1_in, w_ffn1_out, mix_norm, w_in, w_conv, w_conv_out,
           w_attn_out, w_o, ffn2_norm, w_ffn2_in, w_ffn2_out, final_norm):
    batch, seq, d = x.shape
    depth = w_in.shape[0]
    xt = x.reshape(batch * seq, d)
    cos_tab, sin_tab = _rotary_tables(seq)
    final_gain = final_norm.reshape(1, d)
    q_end, xbc_end, gc_end = QKV_COLS, QKV_COLS + XBC_COLS, QKV_COLS + XBC_COLS + D_MODEL
    for l in range(depth):
        xt, w_in_l = _ffn(xt, ffn1_norm[l].reshape(1, d), w_ffn1_in[l].astype(BF16),
                          w_ffn1_out[l].astype(BF16), final_gain, [(w_in[l], (64, 2944))],
                          final_norm=False, tm=1024, tf=512)
        mix_gain = mix_norm[l].reshape(1, d)
        qkv_groups = _qkv(xt, mix_gain, w_in_l, cos_tab, sin_tab, batch=batch, seq=seq, tm=512)
        casts_per_group = (
            [(w_ffn2_in[l], (64, 2 * D_FF))],
            [(w_ffn2_out[l], (176, D_MODEL))],
            [(w_o[l], (64, D_MODEL)), (w_conv_out[l], (32, D_MODEL)),
             (w_attn_out[l], (16, D_MODEL))],
        )
        outs, lses, cast_out = [], [], []
        for qkv_g, (window, dilation), casts in zip(qkv_groups, ATTN_PATTERNS, casts_per_group):
            assert window // dilation == ATTN_BLOCK
            o, lse, *w16 = _attn_group(qkv_g, casts, dilation=dilation, tq=1024)
            outs.append(o)
            lses.append(lse)
            cast_out.extend(w16)
        w_ffn2_in16, w_ffn2_out16, w_o16, w_co16, w_ao16 = cast_out
        conv = _conv_branch(xt, mix_gain, w_in_l, w_co16, w_conv[l], xbc_col0=q_end,
                            gc_col0=xbc_end, seq=seq, tm=512, chunk=512)
        xt = _merge(xt, mix_gain, conv, outs, lses, w_in_l, w_ao16, w_o16, ga_col0=gc_end,
                    seq=seq, tm=512, chunk=512)
        (xt,) = _ffn(xt, ffn2_norm[l].reshape(1, d), w_ffn2_in16, w_ffn2_out16, final_gain,
                     final_norm=(l == depth - 1), tm=1024, tf=512)
    return xt.reshape(batch, seq, d)
```

```python
import functools

import jax
import jax.numpy as jnp
from jax import lax
from jax.experimental import pallas as pl
from jax.experimental.pallas import tpu as pltpu

D_MODEL = 2048
HEAD_DIM = 128
HEADS_PER_GROUP = 4
ATTN_PATTERNS = ((128, 1), (512, 4), (2048, 16))
N_GROUPS = len(ATTN_PATTERNS)
ATTN_WIDTH = N_GROUPS * HEADS_PER_GROUP * HEAD_DIM
GROUP_WIDTH = HEADS_PER_GROUP * HEAD_DIM
ATTN_BLOCK = 128
ROPE_THETA = 500000.0
ROPE_DIM = HEAD_DIM // 4
CONV_WIDTH = D_MODEL // 2
CONV_KERNEL = 3
D_FF = 5632
NORM_EPS = 1e-5
QKV_COLS = 3 * ATTN_WIDTH
XBC_COLS = 3 * CONV_WIDTH

V7X_VMEM_LIMIT_BYTES = 58 * 1024 * 1024
LSE_LANES = 128
LSE_LANES_PER_HEAD = LSE_LANES // HEADS_PER_GROUP
CONV_HALO_ROWS = 8

BF16 = jnp.bfloat16
F32 = jnp.float32


def _rmsnorm_f32(x, g):
    return (x * lax.rsqrt(jnp.mean(x * x, axis=-1, keepdims=True) + NORM_EPS)) * g


def _store_rmsnorm(x_ref, g_ref, h_ref):
    h_ref[...] = _rmsnorm_f32(x_ref[...], g_ref[...]).astype(BF16)


def _compiler_params(n_axes):
    return pltpu.CompilerParams(
        dimension_semantics=("arbitrary",) * n_axes,
        vmem_limit_bytes=V7X_VMEM_LIMIT_BYTES,
    )


def _resident(shape):
    return pl.BlockSpec(shape, lambda *_: (0,) * len(shape), pipeline_mode=pl.Buffered(1))


def _resident_cols(width, col0):
    return pl.BlockSpec((pl.Element(D_MODEL), pl.Element(width)), lambda *_: (0, col0),
                        pipeline_mode=pl.Buffered(1))


def _cast_stream_specs(shape, block, n_steps, step_of):
    n_col_blocks = shape[1] // block[1]
    n_blocks = (shape[0] // block[0]) * n_col_blocks
    assert shape[0] % block[0] == 0 and shape[1] % block[1] == 0 and n_blocks <= n_steps

    def index(*ids):
        b = jnp.minimum(step_of(*ids), n_blocks - 1)
        return (b // n_col_blocks, b % n_col_blocks)

    return pl.BlockSpec(block, index), pl.BlockSpec(block, index)


def _cast_blocks(src_refs, dst_refs):
    for src, dst in zip(src_refs, dst_refs):
        dst[...] = src[...].astype(BF16)


def _ffn_kernel(x_ref, g_ref, wg_ref, wu_ref, wo_ref, gf_ref, *refs, final_norm, n_casts):
    cast_src, (out_ref, *cast_dst), h_ref = refs[:n_casts], refs[n_casts:-1], refs[-1]
    j = pl.program_id(1)

    @pl.when(j == 0)
    def _():
        x = x_ref[...]
        h_ref[...] = _rmsnorm_f32(x, g_ref[...]).astype(BF16)
        out_ref[...] = x

    gate = jnp.dot(h_ref[...], wg_ref[...], preferred_element_type=F32)
    up = jnp.dot(h_ref[...], wu_ref[...], preferred_element_type=F32)
    a = (0.5 * (gate * jax.nn.sigmoid(gate)) * up).astype(BF16)
    out_ref[...] += jnp.dot(a, wo_ref[...], preferred_element_type=F32)
    _cast_blocks(cast_src, cast_dst)

    if final_norm:

        @pl.when(j == pl.num_programs(1) - 1)
        def _():
            out_ref[...] = _rmsnorm_f32(out_ref[...], gf_ref[...])


def _ffn(x, gain, w_in, w_out, final_gain, casts=(), *, final_norm, tm, tf):
    t = x.shape[0]
    nf = D_FF // tf
    n_steps = (t // tm) * nf
    cast_specs = [_cast_stream_specs(w.shape, blk, n_steps, lambda i, j: i * nf + j)
                  for w, blk in casts]
    return pl.pallas_call(
        functools.partial(_ffn_kernel, final_norm=final_norm, n_casts=len(casts)),
        grid=(t // tm, nf),
        in_specs=[
            pl.BlockSpec((tm, D_MODEL), lambda i, j: (i, 0)),
            pl.BlockSpec((1, D_MODEL), lambda i, j: (0, 0)),
            pl.BlockSpec((D_MODEL, tf), lambda i, j: (0, j)),
            pl.BlockSpec((D_MODEL, tf), lambda i, j: (0, j + nf)),
            pl.BlockSpec((tf, D_MODEL), lambda i, j: (j, 0)),
            pl.BlockSpec((1, D_MODEL), lambda i, j: (0, 0)),
            *[src for src, _ in cast_specs],
        ],
        out_specs=[pl.BlockSpec((tm, D_MODEL), lambda i, j: (i, 0)),
                   *[dst for _, dst in cast_specs]],
        out_shape=[jax.ShapeDtypeStruct((t, D_MODEL), F32),
                   *[jax.ShapeDtypeStruct(w.shape, BF16) for w, _ in casts]],
        scratch_shapes=[pltpu.VMEM((tm, D_MODEL), BF16)],
        compiler_params=_compiler_params(2),
        name="ffn_final" if final_norm else "ffn",
    )(x, gain, w_in, w_in, w_out, final_gain, *[w for w, _ in casts])


def _regroup_rows(ref, dilation):
    n = ref.shape[0] // dilation
    return jnp.concatenate(
        [ref[pl.ds(r, n, stride=dilation), :] for r in range(dilation)], axis=0)


def _qkv_kernel(x_ref, g_ref, w_ref, cos_ref, sin_ref, *refs):
    out_refs, hslab_ref, h_ref = refs[:N_GROUPS], refs[N_GROUPS], refs[N_GROUPS + 1]
    tm = x_ref.shape[0]
    lane_blocks = D_MODEL // HEAD_DIM
    hf = _rmsnorm_f32(x_ref[...], g_ref[...])
    for cb in range(lane_blocks):
        hslab_ref[cb] = hf[:, cb * HEAD_DIM:(cb + 1) * HEAD_DIM]
    first_half = lax.broadcasted_iota(jnp.int32, (tm, HEAD_DIM), 1) < ROPE_DIM // 2
    for g, (_, dilation) in enumerate(ATTN_PATTERNS):
        if dilation == 1:
            h_ref[g] = hf.astype(BF16)
            c, s = cos_ref[...], sin_ref[...]
        else:
            for cb in range(lane_blocks):
                h_ref[g, :, cb * HEAD_DIM:(cb + 1) * HEAD_DIM] = (
                    _regroup_rows(hslab_ref.at[cb], dilation).astype(BF16))
            c = _regroup_rows(cos_ref, dilation)
            s = _regroup_rows(sin_ref, dilation)
        n = tm // dilation
        for kind in range(3):
            col0 = kind * ATTN_WIDTH + g * GROUP_WIDTH
            acc = jnp.dot(h_ref[g], w_ref[:, col0:col0 + GROUP_WIDTH],
                          preferred_element_type=F32)
            for hh in range(HEADS_PER_GROUP):
                xh = acc[:, hh * HEAD_DIM:(hh + 1) * HEAD_DIM]
                if kind < 2:
                    nxt = pltpu.roll(xh, HEAD_DIM - ROPE_DIM // 2, axis=1)
                    prv = pltpu.roll(xh, ROPE_DIM // 2, axis=1)
                    xh = xh * c + jnp.where(first_half, nxt, prv) * s
                cols = slice(kind * GROUP_WIDTH + hh * HEAD_DIM, kind * GROUP_WIDTH + (hh + 1) * HEAD_DIM)
                for r in range(dilation):
                    out_refs[g][r, :, cols] = xh[r * n:(r + 1) * n].astype(BF16)


def _qkv(x1, gain, w_in, cos_tab, sin_tab, *, batch, seq, tm):
    t = x1.shape[0]
    tiles_per_seq = seq // tm
    return pl.pallas_call(
        _qkv_kernel,
        grid=(t // tm,),
        in_specs=[
            pl.BlockSpec((tm, D_MODEL), lambda i: (i, 0)),
            _resident((1, D_MODEL)),
            _resident_cols(QKV_COLS, 0),
            pl.BlockSpec((tm, HEAD_DIM), lambda i: (i % tiles_per_seq, 0)),
            pl.BlockSpec((tm, HEAD_DIM), lambda i: (i % tiles_per_seq, 0)),
        ],
        out_specs=[
            pl.BlockSpec((None, d, tm // d, 3 * GROUP_WIDTH),
                         lambda i: (i // tiles_per_seq, 0, i % tiles_per_seq, 0))
            for _, d in ATTN_PATTERNS
        ],
        out_shape=[
            jax.ShapeDtypeStruct((batch, d, seq // d, 3 * GROUP_WIDTH), BF16)
            for _, d in ATTN_PATTERNS
        ],
        scratch_shapes=[
            pltpu.VMEM((D_MODEL // HEAD_DIM, tm, HEAD_DIM), F32),
            pltpu.VMEM((N_GROUPS, tm, D_MODEL), BF16),
        ],
        compiler_params=_compiler_params(1),
        name="qkv_rotary",
    )(x1, gain, w_in, cos_tab, sin_tab)


def _attn_kernel(q_ref, kc_ref, kp_ref, vc_ref, vp_ref, *refs, n_casts):
    cast_src, (o_ref, lse_ref, *cast_dst) = refs[:n_casts], refs[n_casts:]
    _cast_blocks(cast_src, cast_dst)
    n = pl.program_id(2)
    tq = q_ref.shape[0]
    scale = HEAD_DIM ** -0.5
    qi = lax.broadcasted_iota(jnp.int32, (ATTN_BLOCK, 2 * ATTN_BLOCK), 0)
    kj = lax.broadcasted_iota(jnp.int32, (ATTN_BLOCK, 2 * ATTN_BLOCK), 1)
    band = jnp.logical_and(kj >= qi, kj <= qi + ATTN_BLOCK)
    first_band = jnp.logical_and(band, jnp.logical_or(kj >= ATTN_BLOCK, n > 0))
    lane = lax.broadcasted_iota(jnp.int32, (ATTN_BLOCK, LSE_LANES), 1)
    nt = (((1,), (1,)), ((), ()))

    for qb in range(tq // ATTN_BLOCK):
        rows = slice(qb * ATTN_BLOCK, (qb + 1) * ATTN_BLOCK)
        window = slice((qb - 1) * ATTN_BLOCK, (qb + 1) * ATTN_BLOCK)
        lse_tile = jnp.zeros((ATTN_BLOCK, LSE_LANES), F32)
        for hh in range(HEADS_PER_GROUP):
            cols = slice(hh * HEAD_DIM, (hh + 1) * HEAD_DIM)
            q = q_ref[rows, cols]
            if qb == 0:
                k2 = jnp.concatenate([kp_ref[:, cols], kc_ref[rows, cols]], axis=0)
                v2 = jnp.concatenate([vp_ref[:, cols], vc_ref[rows, cols]], axis=0)
                mask = first_band
            else:
                k2 = kc_ref[window, cols]
                v2 = vc_ref[window, cols]
                mask = band
            s = lax.dot_general(q, k2, nt, preferred_element_type=F32) * scale
            s = jnp.where(mask, s, -jnp.inf)
            m = jnp.max(s, axis=-1, keepdims=True)
            p = jnp.exp(s - m)
            l = jnp.sum(p, axis=-1, keepdims=True)
            pv = jnp.dot(p.astype(BF16), v2, preferred_element_type=F32)
            o_ref[rows, cols] = (pv / l).astype(BF16)
            lse_tile = jnp.where(lane // LSE_LANES_PER_HEAD == hh, m + jnp.log(l), lse_tile)
        lse_ref[rows, :] = lse_tile


def _attn_group(qkv_g, casts=(), *, dilation, tq):
    batch, _, strided_len, _ = qkv_g.shape
    tq = min(tq, strided_len)
    sub = tq // ATTN_BLOCK
    n_q = strided_len // tq
    cast_specs = [
        _cast_stream_specs(w.shape, blk, batch * dilation * n_q,
                           lambda b, r, n: (b * dilation + r) * n_q + n)
        for w, blk in casts
    ]

    def cur(kind):
        return pl.BlockSpec((None, None, tq, GROUP_WIDTH), lambda b, r, n: (b, r, n, kind))

    def prev(kind):
        return pl.BlockSpec((None, None, ATTN_BLOCK, GROUP_WIDTH),
                            lambda b, r, n: (b, r, jnp.maximum(n * sub - 1, 0), kind))

    return pl.pallas_call(
        functools.partial(_attn_kernel, n_casts=len(casts)),
        grid=(batch, dilation, n_q),
        in_specs=[cur(0), cur(1), prev(1), cur(2), prev(2), *[src for src, _ in cast_specs]],
        out_specs=[
            pl.BlockSpec((None, None, tq, GROUP_WIDTH), lambda b, r, n: (b, r, n, 0)),
            pl.BlockSpec((None, None, tq, LSE_LANES), lambda b, r, n: (b, r, n, 0)),
            *[dst for _, dst in cast_specs],
        ],
        out_shape=[
            jax.ShapeDtypeStruct((batch, dilation, strided_len, GROUP_WIDTH), BF16),
            jax.ShapeDtypeStruct((batch, dilation, strided_len, LSE_LANES), F32),
            *[jax.ShapeDtypeStruct(w.shape, BF16) for w, _ in casts],
        ],
        compiler_params=_compiler_params(3),
        name=f"dilated_attn_d{dilation}",
    )(qkv_g, qkv_g, qkv_g, qkv_g, qkv_g, *[w for w, _ in casts])


def _conv_kernel(x_ref, g_ref, wxbc_ref, wgc_ref, wco_ref, wconv_ref, out_ref,
                 ubuf_ref, yc_ref, h_ref, *, tiles_per_seq, chunk):
    i = pl.program_id(0)
    tm = x_ref.shape[0]
    halo = CONV_HALO_ROWS

    @pl.when(i % tiles_per_seq == 0)
    def _():
        ubuf_ref[0:halo, :] = jnp.zeros((halo, CONV_WIDTH), F32)

    _store_rmsnorm(x_ref, g_ref, h_ref)
    for c in range(CONV_WIDTH // chunk):
        cols = slice(c * chunk, (c + 1) * chunk)
        xc = jnp.dot(h_ref[...], wxbc_ref[:, cols], preferred_element_type=F32)
        cg = jnp.dot(h_ref[...], wxbc_ref[:, 2 * CONV_WIDTH + c * chunk:2 * CONV_WIDTH + (c + 1) * chunk],
                     preferred_element_type=F32)
        u = cg * xc
        ubuf_ref[halo:halo + tm, cols] = u
        conv = (wconv_ref[2:3, cols] * u
                + wconv_ref[1:2, cols] * ubuf_ref[halo - 1:halo - 1 + tm, cols]
                + wconv_ref[0:1, cols] * ubuf_ref[halo - 2:halo - 2 + tm, cols])
        ubuf_ref[0:halo, cols] = ubuf_ref[tm:tm + halo, cols]
        bg = jnp.dot(h_ref[...], wxbc_ref[:, CONV_WIDTH + c * chunk:CONV_WIDTH + (c + 1) * chunk],
                     preferred_element_type=F32)
        yb = (bg * conv).astype(BF16)
        part = jnp.dot(yb, wco_ref[cols, :], preferred_element_type=F32)
        if c == 0:
            yc_ref[...] = part
        else:
            yc_ref[...] += part
    for c in range(D_MODEL // chunk):
        cols = slice(c * chunk, (c + 1) * chunk)
        gate = jnp.dot(h_ref[...], wgc_ref[:, cols], preferred_element_type=F32)
        out_ref[:, cols] = (jax.nn.sigmoid(gate) * yc_ref[:, cols]).astype(BF16)


def _conv_branch(x1, gain, w_in, w_co, w_conv, *, xbc_col0, gc_col0, seq, tm, chunk):
    t = x1.shape[0]
    return pl.pallas_call(
        functools.partial(_conv_kernel, tiles_per_seq=seq // tm, chunk=chunk),
        grid=(t // tm,),
        in_specs=[
            pl.BlockSpec((tm, D_MODEL), lambda i: (i, 0)),
            _resident((1, D_MODEL)),
            _resident_cols(XBC_COLS, xbc_col0),
            _resident_cols(D_MODEL, gc_col0),
            _resident((CONV_WIDTH, D_MODEL)),
            _resident((CONV_KERNEL, CONV_WIDTH)),
        ],
        out_specs=pl.BlockSpec((tm, D_MODEL), lambda i: (i, 0)),
        out_shape=jax.ShapeDtypeStruct((t, D_MODEL), BF16),
        scratch_shapes=[
            pltpu.VMEM((tm + CONV_HALO_ROWS, CONV_WIDTH), F32),
            pltpu.VMEM((tm, D_MODEL), F32),
            pltpu.VMEM((tm, D_MODEL), BF16),
        ],
        compiler_params=_compiler_params(1),
        name="gated_short_conv",
    )(x1, gain, w_in, w_in, w_co, w_conv)


def _merge_kernel(x_ref, g_ref, conv_ref, o0_ref, o1_ref, o2_ref, l0_ref, l1_ref, l2_ref,
                  wga_ref, wao_ref, wo_ref, out_ref, onat_ref, lnat_ref, oc_ref, h_ref,
                  *, chunk):
    tm = x_ref.shape[0]
    _store_rmsnorm(x_ref, g_ref, h_ref)

    o_refs = (o0_ref, o1_ref, o2_ref)
    l_refs = (l0_ref, l1_ref, l2_ref)
    lses = []
    for g, (_, dilation) in enumerate(ATTN_PATTERNS):
        if dilation == 1:
            lses.append(l_refs[g][0])
            continue
        for r in range(dilation):
            dst = pl.ds(r, tm // dilation, stride=dilation)
            lnat_ref[g - 1, dst, :] = l_refs[g][r]
            for hh in range(HEADS_PER_GROUP):
                cols = slice(hh * HEAD_DIM, (hh + 1) * HEAD_DIM)
                onat_ref[(g - 1) * HEADS_PER_GROUP + hh, dst, :] = o_refs[g][r, :, cols].astype(F32)
        lses.append(lnat_ref[g - 1])

    m = jnp.maximum(jnp.maximum(lses[0], lses[1]), lses[2])
    es = [jnp.exp(l - m) for l in lses]
    den = es[0] + es[1] + es[2]
    ws = [e / den for e in es]
    for hh in range(HEADS_PER_GROUP):
        cols = slice(hh * HEAD_DIM, (hh + 1) * HEAD_DIM)
        lane0 = hh * LSE_LANES_PER_HEAD
        acc = None
        for g, (_, dilation) in enumerate(ATTN_PATTERNS):
            if dilation == 1:
                o = o_refs[g][0, :, cols].astype(F32)
            else:
                o = onat_ref[(g - 1) * HEADS_PER_GROUP + hh]
            term = ws[g][:, lane0:lane0 + 1] * o
            acc = term if acc is None else acc + term
        oc_ref[:, cols] = acc.astype(BF16)

    for c in range(D_MODEL // chunk):
        cols = slice(c * chunk, (c + 1) * chunk)
        gate = jnp.dot(h_ref[...], wga_ref[:, cols], preferred_element_type=F32)
        ya = jnp.dot(oc_ref[...], wao_ref[:, cols], preferred_element_type=F32)
        merged = (conv_ref[:, cols].astype(F32) + jax.nn.sigmoid(gate) * ya).astype(BF16)
        part = jnp.dot(merged, wo_ref[cols, :], preferred_element_type=F32)
        if c == 0:
            out_ref[...] = x_ref[...] + part
        else:
            out_ref[...] += part


def _merge(x1, gain, conv, outs, lses, w_in, w_ao, w_o, *, ga_col0, seq, tm, chunk):
    t = x1.shape[0]
    tiles_per_seq = seq // tm
    n_dilated = N_GROUPS - 1
    assert ATTN_PATTERNS[0][1] == 1 and all(d > 1 for _, d in ATTN_PATTERNS[1:])
    row =lambda width: pl.BlockSpec((tm, width), lambda i: (i, 0))

    def residue_rows(width):
        return [
            pl.BlockSpec((None, d, tm // d, width),
                         lambda i: (i // tiles_per_seq, 0, i % tiles_per_seq, 0))
            for _, d in ATTN_PATTERNS
        ]

    return pl.pallas_call(
        functools.partial(_merge_kernel, chunk=chunk),
        grid=(t // tm,),
        in_specs=[
            row(D_MODEL), _resident((1, D_MODEL)), row(D_MODEL),
            *residue_rows(GROUP_WIDTH), *residue_rows(LSE_LANES),
            _resident_cols(D_MODEL, ga_col0),
            _resident((GROUP_WIDTH, D_MODEL)),
            _resident((D_MODEL, D_MODEL)),
        ],
        out_specs=row(D_MODEL),
        out_shape=jax.ShapeDtypeStruct((t, D_MODEL), F32),
        scratch_shapes=[
            pltpu.VMEM((n_dilated * HEADS_PER_GROUP, tm, HEAD_DIM), F32),
            pltpu.VMEM((n_dilated, tm, LSE_LANES), F32),
            pltpu.VMEM((tm, GROUP_WIDTH), BF16),
            pltpu.VMEM((tm, D_MODEL), BF16),
        ],
        compiler_params=_compiler_params(1),
        name="gated_merge",
    )(x1, gain, conv, *outs, *lses, w_in, w_ao, w_o)


def _rotary_tables(seq):
    half = ROPE_DIM // 2
    inv_freq = ROPE_THETA ** (-(jnp.arange(half, dtype=F32) * 2.0) / ROPE_DIM)
    rest = HEAD_DIM - ROPE_DIM
    freq = jnp.concatenate([inv_freq, inv_freq, jnp.zeros((rest,), F32)])
    sign = jnp.concatenate([-jnp.ones((half,), F32), jnp.ones((half,), F32),
                            jnp.zeros((rest,), F32)])
    ang = jnp.arange(seq, dtype=jnp.int32).astype(F32)[:, None] * freq[None, :]
    return jnp.cos(ang), jnp.sin(ang) * sign[None, :]


@jax.jit
def kernel(x, ffn1_norm, w_ffn1_in, w_ffn1_out, mix_norm, w_in, w_conv, w_conv_out,
           w_attn_out, w_o, ffn2_norm, w_ffn2_in, w_ffn2_out, final_norm):
    batch, seq, d = x.shape
    depth = w_in.shape[0]
    xt = x.reshape(batch * seq, d)
    cos_tab, sin_tab = _rotary_tables(seq)
    final_gain = final_norm.reshape(1, d)
    q_end, xbc_end, gc_end = QKV_COLS, QKV_COLS + XBC_COLS, QKV_COLS + XBC_COLS + D_MODEL
    for l in range(depth):
        xt, w_in_l = _ffn(xt, ffn1_norm[l].reshape(1, d), w_ffn1_in[l].astype(BF16),
                          w_ffn1_out[l].astype(BF16), final_gain, [(w_in[l], (64, 2944))],
                          final_norm=False, tm=1024, tf=512)
        mix_gain = mix_norm[l].reshape(1, d)
        qkv_groups = _qkv(xt, mix_gain, w_in_l, cos_tab, sin_tab, batch=batch, seq=seq, tm=512)
        casts_per_group = (
            [(w_ffn2_in[l], (64, 2 * D_FF))],
            [(w_ffn2_out[l], (176, D_MODEL))],
            [(w_o[l], (64, D_MODEL)), (w_conv_out[l], (32, D_MODEL)),
             (w_attn_out[l], (16, D_MODEL))],
        )
        outs, lses, cast_out = [], [], []
        for qkv_g, (window, dilation), casts in zip(qkv_groups, ATTN_PATTERNS, casts_per_group):
            assert window // dilation == ATTN_BLOCK
            o, lse, *w16 = _attn_group(qkv_g, casts, dilation=dilation, tq=1024)
            outs.append(o)
            lses.append(lse)
            cast_out.extend(w16)
        w_ffn2_in16, w_ffn2_out16, w_o16, w_co16, w_ao16 = cast_out
        conv = _conv_branch(xt, mix_gain, w_in_l, w_co16, w_conv[l], xbc_col0=q_end,
                            gc_col0=xbc_end, seq=seq, tm=512, chunk=512)
        xt = _merge(xt, mix_gain, conv, outs, lses, w_in_l, w_ao16, w_o16, ga_col0=gc_end,
                    seq=seq, tm=512, chunk=512)
        (xt,) = _ffn(xt, ffn2_norm[l].reshape(1, d), w_ffn2_in16, w_ffn2_out16, final_gain,
                     final_norm=(l == depth - 1), tm=1024, tf=512)
    return xt.reshape(batch, seq, d)
```
